```python
import numpy as np
import jax, jax.numpy as jnp
from jax import lax

D_MODEL = 1024
BATCH = 2
SEQ = 8192
DEPTH = 1

D_CONV = D_MODEL
CONV_WIDTH = 3
HEAD_DIM = 64
HEADS_PER_GROUP = 8
DILATED_GROUPS = ((128, 1), (512, 4), (2048, 16))
N_ATTN_HEADS = HEADS_PER_GROUP * len(DILATED_GROUPS)
D_ATTN = N_ATTN_HEADS * HEAD_DIM
D_ATTN_OUT = HEADS_PER_GROUP * HEAD_DIM
BLOCK = 128
ROT_DIM = HEAD_DIM // 4
ROPE_THETA = 500000.0
PEER_HEADS = 8
PEER_N_KEYS = 128
PEER_N_EXPERTS = PEER_N_KEYS * PEER_N_KEYS
PEER_TOPK = 16
PEER_D_HALF = 128
PEER_D_QUERY = 2 * PEER_D_HALF
PEER_CHUNK = 128
RMS_EPS = 1e-6

kernel_name = "hybrid_gatedconv_dilatedattn_peer"


def rms_norm(x, g):
    xf = x.astype(jnp.float32)
    y = xf * lax.rsqrt(jnp.mean(xf * xf, axis=-1, keepdims=True) + RMS_EPS)
    return (y * g.astype(jnp.float32)).astype(x.dtype)


def partial_rotary(t, positions):
    half = ROT_DIM // 2
    inv_freq = ROPE_THETA ** (-jnp.arange(half, dtype=jnp.float32) * (2.0 / ROT_DIM))
    ang = positions.astype(jnp.float32)[..., None] * inv_freq
    cos = jnp.cos(ang)[:, :, None, :]
    sin = jnp.sin(ang)[:, :, None, :]
    tr = t[..., :ROT_DIM].astype(jnp.float32)
    t1, t2 = tr[..., :half], tr[..., half:]
    rot = jnp.concatenate([t1 * cos - t2 * sin, t2 * cos + t1 * sin], axis=-1).astype(t.dtype)
    return jnp.concatenate([rot, t[..., ROT_DIM:]], axis=-1)


def dilated_window_attention(q, k, v, window, dilation):
    B, S, H, Dh = q.shape
    steps = window // dilation
    span = dilation * BLOCK
    sp = -(-S // span) * span
    nb = sp // span

    def blocks(t):
        t = jnp.pad(t, ((0, 0), (0, sp - S), (0, 0), (0, 0)))
        return t.reshape(B, nb, BLOCK, dilation, H, Dh)

    def with_prev(t):
        prev = jnp.pad(t, ((0, 0), (1, 0), (0, 0), (0, 0), (0, 0), (0, 0)))[:, :-1]
        return jnp.concatenate([prev, t], axis=2)

    qb = blocks(q)
    kk = with_prev(blocks(k))
    vv = with_prev(blocks(v))
    s = jnp.einsum('bnqrhd,bnkrhd->bnrhqk', qb, kk,
                   preferred_element_type=jnp.float32) * (Dh ** -0.5)
    qi = np.arange(BLOCK)[:, None]
    ki = np.arange(2 * BLOCK)[None, :]
    dist = BLOCK + qi - ki
    band = (dist >= 0) & (dist <= steps)
    valid = band[None] & ((np.arange(nb)[:, None, None] > 0) | (ki >= BLOCK)[None])
    s = jnp.where(valid[None, :, None, None], s, -jnp.inf)
    m = jnp.max(s, axis=-1, keepdims=True)
    p = jnp.exp(s - m)
    l = jnp.sum(p, axis=-1, keepdims=True)
    o = jnp.einsum('bnrhqk,bnkrhd->bnqrhd', (p / l).astype(v.dtype), vv)
    o = o.reshape(B, sp, H, Dh)[:, :S]
    lse = (m + jnp.log(l))[..., 0]
    lse = lse.transpose(0, 1, 4, 2, 3).reshape(B, sp, H)[:, :S]
    return o, lse


def hybrid_mixer(u, positions, w_in, conv_w, w_conv_out, w_attn_out, gate_bias, w_out):
    B, S, _ = u.shape
    sizes = [D_CONV, D_CONV, D_CONV, D_ATTN, D_ATTN, D_ATTN, D_MODEL, D_MODEL]
    split_points = [int(c) for c in np.cumsum(sizes)[:-1]]
    proj = u @ w_in
    b_gate, c_gate, xc, q, k, v, g_conv, g_attn = jnp.split(proj, split_points, axis=-1)

    z = c_gate * xc
    zp = jnp.pad(z, ((0, 0), (CONV_WIDTH - 1, 0), (0, 0)))
    conv = sum(conv_w[i] * zp[:, i:i + S] for i in range(CONV_WIDTH))
    y_conv = (b_gate * conv) @ w_conv_out

    q = partial_rotary(q.reshape(B, S, N_ATTN_HEADS, HEAD_DIM), positions)
    k = partial_rotary(k.reshape(B, S, N_ATTN_HEADS, HEAD_DIM), positions)
    v = v.reshape(B, S, N_ATTN_HEADS, HEAD_DIM)
    outs, lses = [], []
    for g, (window, dilation) in enumerate(DILATED_GROUPS):
        sl = slice(g * HEADS_PER_GROUP, (g + 1) * HEADS_PER_GROUP)
        o_g, lse_g = dilated_window_attention(q[:, :, sl], k[:, :, sl], v[:, :, sl], window, dilation)
        outs.append(o_g)
        lses.append(lse_g)
    wts = jax.nn.softmax(jnp.stack(lses, axis=0), axis=0)
    o = jnp.einsum('gbsh,gbshd->bshd', wts.astype(v.dtype), jnp.stack(outs, axis=0))
    y_attn = o.reshape(B, S, D_ATTN_OUT) @ w_attn_out

    merged = (jax.nn.sigmoid(g_conv + gate_bias[0]) * y_conv
              + jax.nn.sigmoid(g_attn + gate_bias[1]) * y_attn)
    return merged @ w_out


def peer_ffn(u, w_query, sub_keys, expert_down, expert_up):
    B, S, D = u.shape
    T = B * S
    xf = u.reshape(T, D)
    q = (xf @ w_query).reshape(T, PEER_HEADS, 2, PEER_D_HALF)
    s = jnp.einsum('thcd,hckd->thck', q, sub_keys, preferred_element_type=jnp.float32)
    top_s, top_i = lax.top_k(s, PEER_TOPK)
    cand_s = (top_s[:, :, 0, :, None] + top_s[:, :, 1, None, :]).reshape(T, PEER_HEADS, PEER_TOPK * PEER_TOPK)
    cand_i = (top_i[:, :, 0, :, None] * PEER_N_KEYS + top_i[:, :, 1, None, :]).reshape(T, PEER_HEADS, PEER_TOPK * PEER_TOPK)
    best_s, pos = lax.top_k(cand_s, PEER_TOPK)
    idx = jnp.take_along_axis(cand_i, pos, axis=-1)
    gates = jax.nn.softmax(best_s, axis=-1).astype(u.dtype)
    hk = PEER_HEADS * PEER_TOPK
    xr = xf.reshape(-1, PEER_CHUNK, D)
    ir = idx.reshape(-1, PEER_CHUNK, hk)
    gr = gates.reshape(-1, PEER_CHUNK, hk)

    def chunk(args):
        xc, ic, gc = args
        a = jax.nn.gelu(jnp.einsum('cd,ced->ce', xc, expert_down[ic]), approximate=False)
        return jnp.einsum('ce,ced->cd', gc * a, expert_up[ic])

    return lax.map(chunk, (xr, ir, gr)).reshape(B, S, D)


def setup_inputs(seed: int = 0) -> dict:
    key = jax.random.key(seed)
    ks = jax.random.split(key, 16)
    f32 = jnp.float32
    d_in = 3 * D_CONV + 3 * D_ATTN + 2 * D_MODEL
    x = jax.random.normal(ks[0], (BATCH, SEQ, D_MODEL), f32)
    offset = jax.random.randint(ks[1], (BATCH, 1), 0, 1024, dtype=jnp.int32)
    positions = (offset + jnp.arange(SEQ, dtype=jnp.int32)[None, :]).astype(jnp.int32)
    norm_mix = 1.0 + 0.01 * jax.random.normal(ks[2], (DEPTH, D_MODEL), f32)
    w_in = jax.random.normal(ks[3], (DEPTH, D_MODEL, d_in), f32) * D_MODEL ** -0.5
    conv_w = jax.random.normal(ks[4], (DEPTH, CONV_WIDTH, D_CONV), f32) * CONV_WIDTH ** -0.5
    w_conv_out = jax.random.normal(ks[5], (DEPTH, D_CONV, D_MODEL), f32) * D_CONV ** -0.5
    w_attn_out = jax.random.normal(ks[6], (DEPTH, D_ATTN_OUT, D_MODEL), f32) * D_ATTN_OUT ** -0.5
    gate_bias = 0.01 * jax.random.normal(ks[7], (DEPTH, 2, D_MODEL), f32)
    w_out = jax.random.normal(ks[8], (DEPTH, D_MODEL, D_MODEL), f32) * D_MODEL ** -0.5
    norm_ffn = 1.0 + 0.01 * jax.random.normal(ks[9], (DEPTH, D_MODEL), f32)
    peer_w_query = jax.random.normal(ks[10], (DEPTH, D_MODEL, PEER_HEADS * PEER_D_QUERY), f32) * D_MODEL ** -0.5
    peer_sub_keys = jax.random.normal(ks[11], (DEPTH, PEER_HEADS, 2, PEER_N_KEYS, PEER_D_HALF), f32) * PEER_D_HALF ** -0.5
    peer_down = jax.random.normal(ks[12], (DEPTH, PEER_N_EXPERTS, D_MODEL), f32) * D_MODEL ** -0.5
    peer_up = jax.random.normal(ks[13], (DEPTH, PEER_N_EXPERTS, D_MODEL), f32) * PEER_HEADS ** -0.5
    final_norm = 1.0 + 0.01 * jax.random.normal(ks[14], (D_MODEL,), f32)
    return {"x": x, "positions": positions, "norm_mix": norm_mix, "w_in": w_in,
            "conv_w": conv_w, "w_conv_out": w_conv_out, "w_attn_out": w_attn_out,
            "gate_bias": gate_bias, "w_out": w_out, "norm_ffn": norm_ffn,
            "peer_w_query": peer_w_query, "peer_sub_keys": peer_sub_keys,
            "peer_down": peer_down, "peer_up": peer_up, "final_norm": final_norm}


def reference(x, positions, norm_mix, w_in, conv_w, w_conv_out, w_attn_out, gate_bias, w_out,
              norm_ffn, peer_w_query, peer_sub_keys, peer_down, peer_up, final_norm):
    h = x
    for layer in range(DEPTH):
        h = h + hybrid_mixer(rms_norm(h, norm_mix[layer]), positions, w_in[layer], conv_w[layer],
                             w_conv_out[layer], w_attn_out[layer], gate_bias[layer], w_out[layer])
        h = h + peer_ffn(rms_norm(h, norm_ffn[layer]), peer_w_query[layer], peer_sub_keys[layer],
                         peer_down[layer], peer_up[layer])
    return rms_norm(h, final_norm)
```

```python
import functools

import numpy as np
import jax
import jax.numpy as jnp
from jax import lax
from jax.experimental import pallas as pl
from jax.experimental.pallas import tpu as pltpu

F32 = jnp.float32
BF16 = jnp.bfloat16

D_MODEL = 1024
HEAD_DIM = 64
HEADS_PER_GROUP = 8
DILATED_GROUPS = ((128, 1), (512, 4), (2048, 16))
N_GROUPS = len(DILATED_GROUPS)
D_GROUP = HEADS_PER_GROUP * HEAD_DIM
D_ATTN = N_GROUPS * D_GROUP
BLOCK = 128
ROT_DIM = HEAD_DIM // 4
ROT_HALF = ROT_DIM // 2
ROPE_THETA = 500000.0
PEER_HEADS = 8
PEER_N_KEYS = 128
PEER_TOPK = 16
PEER_D_HALF = 128
RMS_EPS = 1e-6
LANES = 128
VMEM_LIMIT = 56 * 1024 * 1024

COL_B, COL_C, COL_X, COL_GC, COL_GA, COL_QKV = 0, 1024, 2048, 3072, 4096, 5120
D_IN = COL_QKV + 3 * D_ATTN

TM_PROJ = 512
TM_MERGE = 512
TM_ROUTE = 512
TM_DENSE = 512
TE_DENSE = 1024


def _const_spec(shape):
    nd = len(shape)
    return pl.BlockSpec(shape, lambda *_: (0,) * nd, pipeline_mode=pl.Buffered(1))


def _rms(x, g):
    return x * lax.rsqrt(jnp.mean(x * x, axis=-1, keepdims=True) + RMS_EPS) * g


def _proj_kernel(x_ref, pos_ref, nm_ref, invf_ref, w_ref, cw_ref, gb_ref, wco_ref,
                 yc_ref, ga_ref, qkv_ref, zbuf_ref, *, tm, seq):
    i = pl.program_id(0)
    u = _rms(x_ref[...], nm_ref[...]).astype(BF16)

    def mm(c0, n):
        return jnp.dot(u, w_ref[:, c0:c0 + n], preferred_element_type=F32)

    @pl.when((i * tm) % seq == 0)
    def _():
        zbuf_ref[0:8, :] = jnp.zeros((8, D_MODEL), F32)

    z = mm(COL_C, D_MODEL) * mm(COL_X, D_MODEL)
    zbuf_ref[8:tm + 8, :] = z
    cw = cw_ref[...]
    conv = cw[0:1] * zbuf_ref[6:tm + 6, :] + cw[1:2] * zbuf_ref[7:tm + 7, :] + cw[2:3] * z
    zbuf_ref[0:8, :] = zbuf_ref[tm:tm + 8, :]
    yb = (mm(COL_B, D_MODEL) * conv).astype(BF16)
    yc = jnp.dot(yb, wco_ref[...], preferred_element_type=F32)
    yc_ref[...] = (jax.nn.sigmoid(mm(COL_GC, D_MODEL) + gb_ref[0:1, :]) * yc).astype(BF16)
    ga_ref[...] = mm(COL_GA, D_MODEL).astype(BF16)

    ang = pos_ref[...].astype(F32) * invf_ref[...]
    cosv = jnp.cos(ang)
    sinv = jnp.sin(ang)
    lane = lax.broadcasted_iota(jnp.int32, (1, LANES), 1) % HEAD_DIM
    sin_lo = jnp.where(lane < ROT_HALF, -sinv, 0.0)
    sin_hi = jnp.where((lane >= ROT_HALF) & (lane < ROT_DIM), sinv, 0.0)
    for blk in range(2 * N_GROUPS):
        t = mm(COL_QKV + blk * D_GROUP, D_GROUP)
        for s in range(D_GROUP // LANES):
            ts = t[:, s * LANES:(s + 1) * LANES]
            rot = (ts * cosv + pltpu.roll(ts, LANES - ROT_HALF, 1) * sin_lo
                   + pltpu.roll(ts, ROT_HALF, 1) * sin_hi)
            c0 = blk * D_GROUP + s * LANES
            qkv_ref[:, c0:c0 + LANES] = rot.astype(BF16)
    for blk in range(2 * N_GROUPS, 3 * N_GROUPS):
        c0 = blk * D_GROUP
        qkv_ref[:, c0:c0 + D_GROUP] = mm(COL_QKV + c0, D_GROUP).astype(BF16)


def _proj_call(x2, pos2, nm, invf, w_p, cw, gb, wco, *, seq):
    t = x2.shape[0]
    tm = TM_PROJ
    row = lambda n: pl.BlockSpec((tm, n), lambda i: (i, 0))
    return pl.pallas_call(
        functools.partial(_proj_kernel, tm=tm, seq=seq),
        grid=(t // tm,),
        in_specs=[row(D_MODEL), row(1), _const_spec((1, D_MODEL)), _const_spec((1, LANES)),
                  _const_spec((D_MODEL, D_IN)), _const_spec((3, D_MODEL)),
                  _const_spec((2, D_MODEL)), _const_spec((D_MODEL, D_MODEL))],
        out_specs=[row(D_MODEL), row(D_MODEL), row(3 * D_ATTN)],
        out_shape=[jax.ShapeDtypeStruct((t, D_MODEL), BF16),
                   jax.ShapeDtypeStruct((t, D_MODEL), BF16),
                   jax.ShapeDtypeStruct((t, 3 * D_ATTN), BF16)],
        scratch_shapes=[pltpu.VMEM((tm + 8, D_MODEL), F32)],
        compiler_params=pltpu.CompilerParams(dimension_semantics=("arbitrary",),
                                             vmem_limit_bytes=VMEM_LIMIT),
        name="proj",
    )(x2, pos2, nm, invf, w_p, cw, gb, wco)


def _attn_kernel(q_ref, kp_ref, kc_ref, vp_ref, vc_ref, o_ref, lse_ref, *, steps):
    n = pl.program_id(2)
    qi = lax.broadcasted_iota(jnp.int32, (BLOCK, 2 * BLOCK), 0)
    ki = lax.broadcasted_iota(jnp.int32, (BLOCK, 2 * BLOCK), 1)
    dist = BLOCK + qi - ki
    valid = (dist >= 0) & (dist <= steps) & ((n > 0) | (ki >= BLOCK))
    for h in range(HEADS_PER_GROUP):
        sl = slice(h * HEAD_DIM, (h + 1) * HEAD_DIM)
        k = jnp.concatenate([kp_ref[:, sl], kc_ref[:, sl]], axis=0)
        v = jnp.concatenate([vp_ref[:, sl], vc_ref[:, sl]], axis=0)
        s = lax.dot_general(q_ref[:, sl], k, (((1,), (1,)), ((), ())),
                            preferred_element_type=F32) * (HEAD_DIM ** -0.5)
        s = jnp.where(valid, s, -jnp.inf)
        m = jnp.max(s, axis=-1, keepdims=True)
        p = jnp.exp(s - m)
        l = jnp.sum(p, axis=-1, keepdims=True)
        o = jnp.dot(p.astype(BF16), v, preferred_element_type=F32) / l
        o_ref[:, sl] = o
        lse_ref[:, sl] = jnp.broadcast_to(m + jnp.log(l), (BLOCK, HEAD_DIM))


def _attn_call(qkv, g, *, batch, seq):
    window, dil = DILATED_GROUPS[g]
    steps = window // dil
    assert steps <= BLOCK and seq % (dil * BLOCK) == 0
    t = batch * seq
    nb = seq // (dil * BLOCK)
    ncol = 3 * N_GROUPS
    qkv_v = qkv.reshape(t // dil, dil * 3 * D_ATTN)

    def spec(cblk, prev):
        def imap(b, r, n):
            nn = jnp.maximum(n - 1, 0) if prev else n
            return (b * nb + nn, r * ncol + cblk)
        return pl.BlockSpec((BLOCK, D_GROUP), imap)

    out_spec = pl.BlockSpec((BLOCK, D_GROUP), lambda b, r, n: (b * nb + n, r))
    o, lse = pl.pallas_call(
        functools.partial(_attn_kernel, steps=steps),
        grid=(batch, dil, nb),
        in_specs=[spec(g, False), spec(N_GROUPS + g, True), spec(N_GROUPS + g, False),
                  spec(2 * N_GROUPS + g, True), spec(2 * N_GROUPS + g, False)],
        out_specs=[out_spec, out_spec],
        out_shape=[jax.ShapeDtypeStruct((t // dil, dil * D_GROUP), F32)] * 2,
        compiler_params=pltpu.CompilerParams(
            dimension_semantics=("arbitrary", "arbitrary", "arbitrary")),
        name=f"attn{g}",
    )(qkv_v, qkv_v, qkv_v, qkv_v, qkv_v)
    return o.reshape(t, D_GROUP), lse.reshape(t, D_GROUP)


def _merge_kernel(x_ref, yc_ref, ga_ref, o0_ref, o1_ref, o2_ref, l0_ref, l1_ref, l2_ref,
                  gb_ref, wao_ref, wo_ref, h_ref):
    l0, l1, l2 = l0_ref[...], l1_ref[...], l2_ref[...]
    mx = jnp.maximum(jnp.maximum(l0, l1), l2)
    e0, e1, e2 = jnp.exp(l0 - mx), jnp.exp(l1 - mx), jnp.exp(l2 - mx)
    o = (e0 * o0_ref[...] + e1 * o1_ref[...] + e2 * o2_ref[...]) / (e0 + e1 + e2)
    ya = jnp.dot(o.astype(BF16), wao_ref[...], preferred_element_type=F32)
    merged = (yc_ref[...].astype(F32)
              + jax.nn.sigmoid(ga_ref[...].astype(F32) + gb_ref[1:2, :]) * ya)
    h_ref[...] = x_ref[...] + jnp.dot(merged.astype(BF16), wo_ref[...],
                                      preferred_element_type=F32)


def _merge_call(x2, yc, ga, outs, lses, gb, wao, wo):
    t = x2.shape[0]
    tm = TM_MERGE
    row = lambda n: pl.BlockSpec((tm, n), lambda i: (i, 0))
    return pl.pallas_call(
        _merge_kernel,
        grid=(t // tm,),
        in_specs=[row(D_MODEL), row(D_MODEL), row(D_MODEL)] + [row(D_GROUP)] * 6
                 + [_const_spec((2, D_MODEL)), _const_spec((D_GROUP, D_MODEL)),
                    _const_spec((D_MODEL, D_MODEL))],
        out_specs=row(D_MODEL),
        out_shape=jax.ShapeDtypeStruct((t, D_MODEL), F32),
        compiler_params=pltpu.CompilerParams(dimension_semantics=("arbitrary",),
                                             vmem_limit_bytes=VMEM_LIMIT),
        name="merge",
    )(x2, yc, ga, *outs, *lses, gb, wao, wo)


def _pair_candidates():
    return [(a, b) for a in range(PEER_TOPK) for b in range(PEER_TOPK)
            if (a + 1) * (b + 1) <= PEER_TOPK]


def _route_kernel(h_ref, nf_ref, wqt_ref, sk_ref,
                  u2t_ref, s1_ref, s2_ref, e1_ref, e2_ref, tau_ref,
                  qt_ref, ts_ref, *, tm):
    u2 = _rms(h_ref[...], nf_ref[...])
    u2t = u2.T.astype(BF16)
    u2t_ref[...] = u2t
    qt_ref[...] = jnp.dot(wqt_ref[...], u2t, preferred_element_type=F32).astype(BF16)
    s_refs = (s1_ref, s2_ref)

    def head_body(h, carry):
        for c in range(2):
            r0 = pl.multiple_of(h * (2 * PEER_D_HALF) + c * PEER_D_HALF, PEER_D_HALF)
            st = jnp.dot(sk_ref[h, c], qt_ref[pl.ds(r0, PEER_D_HALF), :],
                         preferred_element_type=F32)
            s_refs[c][h] = st
            cur = st
            for r in range(PEER_TOPK):
                m = jnp.max(cur, axis=0, keepdims=True)
                ts_ref[c, r, pl.ds(h, 1), :] = m
                if r + 1 < PEER_TOPK:
                    cur = jnp.where(cur >= m, -jnp.inf, cur)
        return carry

    lax.fori_loop(0, PEER_HEADS, head_body, 0)

    top1 = [ts_ref[0, r] for r in range(PEER_TOPK)]
    top2 = [ts_ref[1, r] for r in range(PEER_TOPK)]
    cands = [top1[a] + top2[b] for a, b in _pair_candidates()]
    cmax = top1[0] + top2[0]
    cur = list(cands)
    for r in range(PEER_TOPK):
        m = functools.reduce(jnp.maximum, cur)
        if r + 1 < PEER_TOPK:
            cur = [jnp.where(c >= m, -jnp.inf, c) for c in cur]
    tau = m
    z = functools.reduce(
        jnp.add, [jnp.where(c >= tau, jnp.exp(c - cmax), 0.0) for c in cands])
    inv_z = 1.0 / z
    tau_ref[...] = tau
    for h in range(PEER_HEADS):
        e1_ref[h] = jnp.exp(s1_ref[h] - top1[0][h:h + 1, :]) * inv_z[h:h + 1, :]
        e2_ref[h] = jnp.exp(s2_ref[h] - top2[0][h:h + 1, :])


def _route_call(h1, nf, wqt, sk):
    t = h1.shape[0]
    tm = TM_ROUTE
    big = pl.BlockSpec((PEER_HEADS, PEER_N_KEYS, tm), lambda i: (0, 0, i))
    big_shape = jax.ShapeDtypeStruct((PEER_HEADS, PEER_N_KEYS, t), F32)
    return pl.pallas_call(
        functools.partial(_route_kernel, tm=tm),
        grid=(t // tm,),
        in_specs=[pl.BlockSpec((tm, D_MODEL), lambda i: (i, 0)), _const_spec((1, D_MODEL)),
                  _const_spec(wqt.shape), _const_spec(sk.shape)],
        out_specs=[pl.BlockSpec((D_MODEL, tm), lambda i: (0, i)), big, big, big, big,
                   pl.BlockSpec((PEER_HEADS, tm), lambda i: (0, i))],
        out_shape=[jax.ShapeDtypeStruct((D_MODEL, t), BF16), big_shape, big_shape, big_shape,
                   big_shape, jax.ShapeDtypeStruct((PEER_HEADS, t), F32)],
        scratch_shapes=[pltpu.VMEM((wqt.shape[0], tm), BF16),
                        pltpu.VMEM((2, PEER_TOPK, PEER_HEADS, tm), F32)],
        compiler_params=pltpu.CompilerParams(dimension_semantics=("arbitrary",),
                                             vmem_limit_bytes=VMEM_LIMIT),
        name="route",
    )(h1, nf, wqt, sk)


def _dense_kernel(u2t_ref, down_ref, upt_ref, s1_ref, s2_ref, e1_ref, e2_ref, tau_ref,
                  h_ref, fn_ref, out_ref, acc_ref, ht_ref, wt_ref, *, tm, te):
    j = pl.program_id(1)
    nc = te // PEER_N_KEYS

    @pl.when(j == 0)
    def _():
        acc_ref[...] = jnp.zeros_like(acc_ref)

    ht_ref[...] = jnp.dot(down_ref[...], u2t_ref[...], preferred_element_type=F32)

    def lane_body(tb, carry):
        ln = pl.ds(pl.multiple_of(tb * LANES, LANES), LANES)
        for cc in range(nc):
            rows = slice(cc * PEER_N_KEYS, (cc + 1) * PEER_N_KEYS)
            g = jnp.zeros((PEER_N_KEYS, LANES), F32)
            for h in range(PEER_HEADS):
                ssum = s2_ref[h, :, ln] + s1_ref[h, cc:cc + 1, ln]
                val = e2_ref[h, :, ln] * e1_ref[h, cc:cc + 1, ln]
                g = g + jnp.where(ssum >= tau_ref[h:h + 1, ln], val, 0.0)
            hv = ht_ref[rows, ln]
            act = 0.5 * hv * (1.0 + lax.erf(hv * np.float32(2.0 ** -0.5)))
            wt_ref[rows, ln] = (g * act).astype(BF16)
        return carry

    lax.fori_loop(0, tm // LANES, lane_body, 0)
    acc_ref[...] += jnp.dot(upt_ref[...], wt_ref[...], preferred_element_type=F32)

    @pl.when(j == pl.num_programs(1) - 1)
    def _():
        out_ref[...] = _rms(h_ref[...] + acc_ref[...].T, fn_ref[...])


def _dense_call(u2t, down, upt, s1, s2, e1, e2, tau, h1, fn):
    t = h1.shape[0]
    n_exp = down.shape[0]
    tm, te = TM_DENSE, TE_DENSE
    assert te // PEER_N_KEYS == 8
    big = pl.BlockSpec((PEER_HEADS, PEER_N_KEYS, tm), lambda i, j: (0, 0, i))
    rows = pl.BlockSpec((PEER_HEADS, te // PEER_N_KEYS, tm), lambda i, j: (0, j, i))
    return pl.pallas_call(
        functools.partial(_dense_kernel, tm=tm, te=te),
        grid=(t // tm, n_exp // te),
        in_specs=[pl.BlockSpec((D_MODEL, tm), lambda i, j: (0, i)),
                  pl.BlockSpec((te, D_MODEL), lambda i, j: (j, 0)),
                  pl.BlockSpec((D_MODEL, te), lambda i, j: (0, j)),
                  rows, big, rows, big,
                  pl.BlockSpec((PEER_HEADS, tm), lambda i, j: (0, i)),
                  pl.BlockSpec((tm, D_MODEL), lambda i, j: (i, 0)),
                  pl.BlockSpec((1, D_MODEL), lambda i, j: (0, 0))],
        out_specs=pl.BlockSpec((tm, D_MODEL), lambda i, j: (i, 0)),
        out_shape=jax.ShapeDtypeStruct((t, D_MODEL), F32),
        scratch_shapes=[pltpu.VMEM((D_MODEL, tm), F32), pltpu.VMEM((te, tm), F32),
                        pltpu.VMEM((te, tm), BF16)],
        compiler_params=pltpu.CompilerParams(dimension_semantics=("arbitrary", "arbitrary"),
                                             vmem_limit_bytes=VMEM_LIMIT),
        name="dense",
    )(u2t, down, upt, s1, s2, e1, e2, tau, h1, fn)


def _rope_lane_freqs():
    inv_freq = ROPE_THETA ** (-jnp.arange(ROT_HALF, dtype=F32) * (2.0 / ROT_DIM))
    lane = np.arange(LANES) % HEAD_DIM
    return jnp.where(lane < ROT_DIM, inv_freq[lane % ROT_HALF], 0.0).reshape(1, LANES).astype(F32)


def kernel(x, positions, norm_mix, w_in, conv_w, w_conv_out, w_attn_out, gate_bias, w_out,
           norm_ffn, peer_w_query, peer_sub_keys, peer_down, peer_up, final_norm):
    batch, seq, d = x.shape
    assert d == D_MODEL and w_in.shape[-1] == D_IN
    depth = w_in.shape[0]
    t = batch * seq
    h = x.reshape(t, d)
    pos2 = positions.reshape(t, 1)
    invf = _rope_lane_freqs()
    n_bcx = 3 * D_MODEL
    for layer in range(depth):
        w = w_in[layer]
        w_p = jnp.concatenate([w[:, :n_bcx], w[:, n_bcx + 3 * D_ATTN:],
                               w[:, n_bcx:n_bcx + 3 * D_ATTN]], axis=1).astype(BF16)
        yc, ga, qkv = _proj_call(h, pos2, norm_mix[layer].reshape(1, d), invf, w_p,
                                 conv_w[layer], gate_bias[layer],
                                 w_conv_out[layer].astype(BF16), seq=seq)
        outs, lses = zip(*[_attn_call(qkv, g, batch=batch, seq=seq) for g in range(N_GROUPS)])
        h1 = _merge_call(h, yc, ga, outs, lses, gate_bias[layer],
                         w_attn_out[layer].astype(BF16), w_out[layer].astype(BF16))
        u2t, s1, s2, e1, e2, tau = _route_call(
            h1, norm_ffn[layer].reshape(1, d), peer_w_query[layer].T.astype(BF16),
            peer_sub_keys[layer].astype(BF16))
        last = layer == depth - 1
        fn = final_norm.reshape(1, d) if last else None
        assert last, "intermediate layers need a dense call without the final norm"
        h = _dense_call(u2t, peer_down[layer].astype(BF16), peer_up[layer].T.astype(BF16),
                        s1, s2, e1, e2, tau, h1, fn)
    return h.reshape(batch, seq, d)
```

```python
import functools

import numpy as np
import jax
import jax.numpy as jnp
from jax import lax
from jax.experimental import pallas as pl
from jax.experimental.pallas import tpu as pltpu

F32 = jnp.float32
BF16 = jnp.bfloat16

D_MODEL = 1024
HEAD_DIM = 64
HEADS_PER_GROUP = 8
DILATED_GROUPS = ((128, 1), (512, 4), (2048, 16))
N_GROUPS = len(DILATED_GROUPS)
D_GROUP = HEADS_PER_GROUP * HEAD_DIM
D_ATTN = N_GROUPS * D_GROUP
BLOCK = 128
ROT_DIM = HEAD_DIM // 4
ROT_HALF = ROT_DIM // 2
ROPE_THETA = 500000.0
PEER_HEADS = 8
PEER_N_KEYS = 128
PEER_TOPK = 16
PEER_D_HALF = 128
RMS_EPS = 1e-6
LANES = 128
SUBLANES = 8
VMEM_LIMIT = 56 * 1024 * 1024

COL_B, COL_C, COL_X, COL_GC, COL_GA, COL_QKV = 0, 1024, 2048, 3072, 4096, 5120
D_IN = COL_QKV + 3 * D_ATTN

TM_PROJ = 512
TM_MERGE = 512
TM_ROUTE = 512
TM_DENSE = 1024
TE_DENSE = SUBLANES * PEER_N_KEYS
QL_DENSE = 256


def _const_spec(shape):
    nd = len(shape)
    return pl.BlockSpec(shape, lambda *_: (0,) * nd, pipeline_mode=pl.Buffered(1))


def _rms(x, g):
    return x * lax.rsqrt(jnp.mean(x * x, axis=-1, keepdims=True) + RMS_EPS) * g


def _proj_kernel(x_ref, pos_ref, nm_ref, invf_ref, w_ref, cw_ref, gb_ref, wco_ref, *rest,
                 tm, seq):
    yc_ref, ga_ref = rest[0], rest[1]
    qkv_refs = rest[2:2 + 3 * N_GROUPS]
    zbuf_ref, t_ref = rest[2 + 3 * N_GROUPS:]
    i = pl.program_id(0)
    u = _rms(x_ref[...], nm_ref[...]).astype(BF16)

    def mm(c0, n):
        return jnp.dot(u, w_ref[:, c0:c0 + n], preferred_element_type=F32)

    @pl.when((i * tm) % seq == 0)
    def _():
        zbuf_ref[0:8, :] = jnp.zeros((8, D_MODEL), F32)

    z = mm(COL_C, D_MODEL) * mm(COL_X, D_MODEL)
    zbuf_ref[8:tm + 8, :] = z
    cw = cw_ref[...]
    conv = cw[0:1] * zbuf_ref[6:tm + 6, :] + cw[1:2] * zbuf_ref[7:tm + 7, :] + cw[2:3] * z
    zbuf_ref[0:8, :] = zbuf_ref[tm:tm + 8, :]
    yb = (mm(COL_B, D_MODEL) * conv).astype(BF16)
    yc = jnp.dot(yb, wco_ref[...], preferred_element_type=F32)
    yc_ref[...] = (jax.nn.sigmoid(mm(COL_GC, D_MODEL) + gb_ref[0:1, :]) * yc).astype(BF16)
    ga_ref[...] = mm(COL_GA, D_MODEL).astype(BF16)

    ang = pos_ref[...].astype(F32) * invf_ref[...]
    cosv = jnp.cos(ang)
    sinv = jnp.sin(ang)
    lane = lax.broadcasted_iota(jnp.int32, (1, LANES), 1) % HEAD_DIM
    sin_lo = jnp.where(lane < ROT_HALF, -sinv, 0.0)
    sin_hi = jnp.where((lane >= ROT_HALF) & (lane < ROT_DIM), sinv, 0.0)
    for blk in range(3 * N_GROUPS):
        t = mm(COL_QKV + blk * D_GROUP, D_GROUP)
        dil = DILATED_GROUPS[blk % N_GROUPS][1]
        out_ref = qkv_refs[blk]
        for s in range(D_GROUP // LANES):
            ts = t[:, s * LANES:(s + 1) * LANES]
            if blk < 2 * N_GROUPS:
                ts = (ts * cosv + pltpu.roll(ts, LANES - ROT_HALF, 1) * sin_lo
                      + pltpu.roll(ts, ROT_HALF, 1) * sin_hi)
            if dil == 1:
                out_ref[0, :, s * LANES:(s + 1) * LANES] = ts.astype(BF16)
            else:
                t_ref[s] = ts
        if dil > 1:
            for r in range(dil):
                for s in range(D_GROUP // LANES):
                    out_ref[r, :, s * LANES:(s + 1) * LANES] = t_ref[
                        s, pl.ds(r, tm // dil, stride=dil), :].astype(BF16)


def _proj_call(x2, pos2, nm, invf, w_p, cw, gb, wco, *, seq):
    t = x2.shape[0]
    tm = TM_PROJ
    row = lambda n: pl.BlockSpec((tm, n), lambda i: (i, 0))
    dils = [d for _, d in DILATED_GROUPS] * 3
    qkv_specs = [pl.BlockSpec((d, tm // d, D_GROUP), lambda i: (0, i, 0)) for d in dils]
    qkv_shapes = [jax.ShapeDtypeStruct((d, t // d, D_GROUP), BF16) for d in dils]
    res = pl.pallas_call(
        functools.partial(_proj_kernel, tm=tm, seq=seq),
        grid=(t // tm,),
        in_specs=[row(D_MODEL), row(1), _const_spec((1, D_MODEL)), _const_spec((1, LANES)),
                  _const_spec((D_MODEL, D_IN)), _const_spec((3, D_MODEL)),
                  _const_spec((2, D_MODEL)), _const_spec((D_MODEL, D_MODEL))],
        out_specs=[row(D_MODEL), row(D_MODEL)] + qkv_specs,
        out_shape=[jax.ShapeDtypeStruct((t, D_MODEL), BF16),
                   jax.ShapeDtypeStruct((t, D_MODEL), BF16)] + qkv_shapes,
        scratch_shapes=[pltpu.VMEM((tm + 8, D_MODEL), F32),
                        pltpu.VMEM((D_GROUP // LANES, tm, LANES), F32)],
        compiler_params=pltpu.CompilerParams(dimension_semantics=("arbitrary",),
                                             vmem_limit_bytes=VMEM_LIMIT),
        name="proj",
    )(x2, pos2, nm, invf, w_p, cw, gb, wco)
    yc, ga = res[0], res[1]
    qs, ks, vs = res[2:2 + N_GROUPS], res[2 + N_GROUPS:2 + 2 * N_GROUPS], res[2 + 2 * N_GROUPS:]
    return yc, ga, qs, ks, vs


def _attn_kernel(q_ref, kp_ref, kc_ref, vp_ref, vc_ref, o_ref, lse_ref, *, steps):
    n = pl.program_id(2)
    qi = lax.broadcasted_iota(jnp.int32, (BLOCK, 2 * BLOCK), 0)
    ki = lax.broadcasted_iota(jnp.int32, (BLOCK, 2 * BLOCK), 1)
    dist = BLOCK + qi - ki
    valid = (dist >= 0) & (dist <= steps) & ((n > 0) | (ki >= BLOCK))
    for h in range(HEADS_PER_GROUP):
        sl = slice(h * HEAD_DIM, (h + 1) * HEAD_DIM)
        k = jnp.concatenate([kp_ref[:, sl], kc_ref[:, sl]], axis=0)
        v = jnp.concatenate([vp_ref[:, sl], vc_ref[:, sl]], axis=0)
        s = lax.dot_general(q_ref[:, sl], k, (((1,), (1,)), ((), ())),
                            preferred_element_type=F32) * (HEAD_DIM ** -0.5)
        s = jnp.where(valid, s, -jnp.inf)
        m = jnp.max(s, axis=-1, keepdims=True)
        p = jnp.exp(s - m)
        l = jnp.sum(p, axis=-1, keepdims=True)
        o = jnp.dot(p.astype(BF16), v, preferred_element_type=F32) / l
        o_ref[:, sl] = o
        lse_ref[:, sl] = jnp.broadcast_to(m + jnp.log(l), (BLOCK, HEAD_DIM))


def _attn_call(q, k, v, g, *, batch, seq):
    window, dil = DILATED_GROUPS[g]
    steps = window // dil
    assert steps <= BLOCK and seq % (dil * BLOCK) == 0
    nb = seq // (dil * BLOCK)

    def spec(prev):
        def imap(b, r, n):
            nn = jnp.maximum(n - 1, 0) if prev else n
            return (r, b * nb + nn, 0)
        return pl.BlockSpec((None, BLOCK, D_GROUP), imap)

    return pl.pallas_call(
        functools.partial(_attn_kernel, steps=steps),
        grid=(batch, dil, nb),
        in_specs=[spec(False), spec(True), spec(False), spec(True), spec(False)],
        out_specs=[spec(False), spec(False)],
        out_shape=[jax.ShapeDtypeStruct(q.shape, F32)] * 2,
        compiler_params=pltpu.CompilerParams(
            dimension_semantics=("arbitrary", "arbitrary", "arbitrary")),
        name=f"attn{g}",
    )(q, k, k, v, v)


def _merge_kernel(x_ref, yc_ref, ga_ref, o0_ref, o1_ref, o2_ref, l0_ref, l1_ref, l2_ref,
                  gb_ref, wao_ref, wo_ref, h_ref, *scratch, tm):
    def natural(ref, g, scr):
        dil = DILATED_GROUPS[g][1]
        if dil == 1:
            return ref[0]
        n_slab = D_GROUP // LANES
        for r in range(dil):
            for s in range(n_slab):
                scr[s, pl.ds(r, tm // dil, stride=dil), :] = ref[r, :, s * LANES:(s + 1) * LANES]
        return jnp.concatenate([scr[s] for s in range(n_slab)], axis=1)

    scr = iter(scratch)
    o_refs, l_refs = (o0_ref, o1_ref, o2_ref), (l0_ref, l1_ref, l2_ref)
    ls = [natural(l_refs[g], g, next(scr) if DILATED_GROUPS[g][1] > 1 else None)
          for g in range(N_GROUPS)]
    os_ = [natural(o_refs[g], g, next(scr) if DILATED_GROUPS[g][1] > 1 else None)
           for g in range(N_GROUPS)]
    mx = functools.reduce(jnp.maximum, ls)
    es = [jnp.exp(l - mx) for l in ls]
    o = sum(e * ov for e, ov in zip(es, os_)) / sum(es)
    ya = jnp.dot(o.astype(BF16), wao_ref[...], preferred_element_type=F32)
    merged = (yc_ref[...].astype(F32)
              + jax.nn.sigmoid(ga_ref[...].astype(F32) + gb_ref[1:2, :]) * ya)
    h_ref[...] = x_ref[...] + jnp.dot(merged.astype(BF16), wo_ref[...],
                                      preferred_element_type=F32)


def _merge_call(x2, yc, ga, outs, lses, gb, wao, wo):
    t = x2.shape[0]
    tm = TM_MERGE
    row = lambda n: pl.BlockSpec((tm, n), lambda i: (i, 0))
    dils = [d for _, d in DILATED_GROUPS]
    grp = [pl.BlockSpec((d, tm // d, D_GROUP), lambda i: (0, i, 0)) for d in dils]
    n_scr = 2 * sum(d > 1 for d in dils)
    return pl.pallas_call(
        functools.partial(_merge_kernel, tm=tm),
        grid=(t // tm,),
        in_specs=[row(D_MODEL), row(D_MODEL), row(D_MODEL)] + grp + grp
                 + [_const_spec((2, D_MODEL)), _const_spec((D_GROUP, D_MODEL)),
                    _const_spec((D_MODEL, D_MODEL))],
        out_specs=row(D_MODEL),
        out_shape=jax.ShapeDtypeStruct((t, D_MODEL), F32),
        scratch_shapes=[pltpu.VMEM((D_GROUP // LANES, tm, LANES), F32)] * n_scr,
        compiler_params=pltpu.CompilerParams(dimension_semantics=("arbitrary",),
                                             vmem_limit_bytes=VMEM_LIMIT),
        name="merge",
    )(x2, yc, ga, *outs, *lses, gb, wao, wo)


def _pair_candidates():
    return [(a, b) for a in range(PEER_TOPK) for b in range(PEER_TOPK)
            if (a + 1) * (b + 1) <= PEER_TOPK]


def _route_kernel(h_ref, nf_ref, wqt_ref, sk_ref,
                  u2t_ref, n_ref, r2_ref, e1_ref, e2_ref,
                  qt_ref, ts_ref, s_ref, rk_ref, *, tm):
    u2 = _rms(h_ref[...], nf_ref[...])
    u2t = u2.T.astype(BF16)
    u2t_ref[...] = u2t
    qt_ref[...] = jnp.dot(wqt_ref[...], u2t, preferred_element_type=F32).astype(BF16)
    not_top = float(PEER_TOPK)

    def head_body(h, carry):
        for c in range(2):
            r0 = pl.multiple_of(h * (2 * PEER_D_HALF) + c * PEER_D_HALF, PEER_D_HALF)
            st = jnp.dot(sk_ref[h, c], qt_ref[pl.ds(r0, PEER_D_HALF), :],
                         preferred_element_type=F32)
            s_ref[c, h] = st
            cur = st
            rank = jnp.full(st.shape, not_top, F32)
            for r in range(PEER_TOPK):
                m = jnp.max(cur, axis=0, keepdims=True)
                ts_ref[c, r, pl.ds(h, 1), :] = m
                hit = cur >= m
                rank = jnp.where(hit, float(r), rank)
                if r + 1 < PEER_TOPK:
                    cur = jnp.where(hit, -jnp.inf, cur)
            rk_ref[c, h] = rank
        return carry

    lax.fori_loop(0, PEER_HEADS, head_body, 0)

    top1 = [ts_ref[0, r] for r in range(PEER_TOPK)]
    top2 = [ts_ref[1, r] for r in range(PEER_TOPK)]
    pairs = _pair_candidates()
    cands = [top1[a] + top2[b] for a, b in pairs]
    cmax = top1[0] + top2[0]
    cur = list(cands)
    for r in range(PEER_TOPK):
        m = functools.reduce(jnp.maximum, cur)
        if r + 1 < PEER_TOPK:
            cur = [jnp.where(c >= m, -jnp.inf, c) for c in cur]
    tau = m
    sel = [c >= tau for c in cands]
    z = functools.reduce(jnp.add, [jnp.where(s, jnp.exp(c - cmax), 0.0)
                                   for s, c in zip(sel, cands)])
    inv_z = 1.0 / z
    cnt = []
    for a in range(PEER_TOPK):
        cnt.append(functools.reduce(
            jnp.add, [jnp.where(s, 1.0, 0.0) for s, (pa, _) in zip(sel, pairs) if pa == a]))

    for h in range(PEER_HEADS):
        row = slice(h, h + 1)
        rank1 = rk_ref[0, h]
        n = jnp.zeros(rank1.shape, F32)
        for a in range(PEER_TOPK):
            n = jnp.where(rank1 == float(a), cnt[a][row, :], n)
        n_ref[h] = n
        e1_ref[h] = jnp.exp(s_ref[0, h] - top1[0][row, :]) * inv_z[row, :]
        r2_ref[h] = rk_ref[1, h].astype(BF16)
        e2_ref[h] = jnp.exp(s_ref[1, h] - top2[0][row, :]).astype(BF16)


def _route_call(h1, nf, wqt, sk):
    t = h1.shape[0]
    tm = TM_ROUTE
    big = pl.BlockSpec((PEER_HEADS, PEER_N_KEYS, tm), lambda i: (0, 0, i))
    shape = lambda dt: jax.ShapeDtypeStruct((PEER_HEADS, PEER_N_KEYS, t), dt)
    return pl.pallas_call(
        functools.partial(_route_kernel, tm=tm),
        grid=(t // tm,),
        in_specs=[pl.BlockSpec((tm, D_MODEL), lambda i: (i, 0)), _const_spec((1, D_MODEL)),
                  _const_spec(wqt.shape), _const_spec(sk.shape)],
        out_specs=[pl.BlockSpec((D_MODEL, tm), lambda i: (0, i)), big, big, big, big],
        out_shape=[jax.ShapeDtypeStruct((D_MODEL, t), BF16), shape(F32), shape(BF16),
                   shape(F32), shape(BF16)],
        scratch_shapes=[pltpu.VMEM((wqt.shape[0], tm), BF16),
                        pltpu.VMEM((2, PEER_TOPK, PEER_HEADS, tm), F32),
                        pltpu.VMEM((2, PEER_HEADS, PEER_N_KEYS, tm), F32),
                        pltpu.VMEM((2, PEER_HEADS, PEER_N_KEYS, tm), F32)],
        compiler_params=pltpu.CompilerParams(dimension_semantics=("arbitrary",),
                                             vmem_limit_bytes=VMEM_LIMIT),
        name="route",
    )(h1, nf, wqt, sk)


def _dense_kernel(u2t_ref, down_ref, upt_ref, n_ref, r2_ref, e1_ref, e2_ref,
                  h_ref, fn_ref, out_ref, acc_ref, ht_ref, wt_ref, *, tm, te, ql):
    j = pl.program_id(1)
    nc = te // PEER_N_KEYS
    nq = tm // ql

    @pl.when(j == 0)
    def _():
        acc_ref[...] = jnp.zeros_like(acc_ref)

    def hidden(q):
        ht_ref[q] = jnp.dot(down_ref[...], u2t_ref[:, q * ql:(q + 1) * ql],
                            preferred_element_type=F32)

    hidden(0)
    for q in range(nq):
        if q + 1 < nq:
            hidden(q + 1)
        ln = slice(q * ql, (q + 1) * ql)
        for cc in range(nc):
            rows = slice(cc * PEER_N_KEYS, (cc + 1) * PEER_N_KEYS)
            g = jnp.zeros((PEER_N_KEYS, ql), BF16)
            for h in range(PEER_HEADS):
                n_row = n_ref[h, cc:cc + 1, ln].astype(BF16)
                e1_row = e1_ref[h, cc:cc + 1, ln].astype(BF16)
                g = g + jnp.where(r2_ref[h, :, ln] < n_row, e2_ref[h, :, ln] * e1_row,
                                  jnp.zeros((), BF16))
            hv = ht_ref[q, rows, :]
            act = 0.5 * hv * (1.0 + lax.erf(hv * np.float32(2.0 ** -0.5)))
            wt_ref[q, rows, :] = g * act.astype(BF16)
        acc_ref[:, ln] += jnp.dot(upt_ref[...], wt_ref[q], preferred_element_type=F32)

    @pl.when(j == pl.num_programs(1) - 1)
    def _():
        out_ref[...] = _rms(h_ref[...] + acc_ref[...].T, fn_ref[...])


def _dense_call(u2t, down, upt, n, r2, e1, e2, h1, fn):
    t = h1.shape[0]
    n_exp = down.shape[0]
    tm, te, ql = TM_DENSE, TE_DENSE, QL_DENSE
    nc = te // PEER_N_KEYS
    big = pl.BlockSpec((PEER_HEADS, PEER_N_KEYS, tm), lambda i, j: (0, 0, i))
    rows = pl.BlockSpec((PEER_HEADS, nc, tm), lambda i, j: (0, j, i))
    return pl.pallas_call(
        functools.partial(_dense_kernel, tm=tm, te=te, ql=ql),
        grid=(t // tm, n_exp // te),
        in_specs=[pl.BlockSpec((D_MODEL, tm), lambda i, j: (0, i)),
                  pl.BlockSpec((te, D_MODEL), lambda i, j: (j, 0)),
                  pl.BlockSpec((D_MODEL, te), lambda i, j: (0, j)),
                  rows, big, rows, big,
                  pl.BlockSpec((tm, D_MODEL), lambda i, j: (i, 0)),
                  pl.BlockSpec((1, D_MODEL), lambda i, j: (0, 0))],
        out_specs=pl.BlockSpec((tm, D_MODEL), lambda i, j: (i, 0)),
        out_shape=jax.ShapeDtypeStruct((t, D_MODEL), F32),
        scratch_shapes=[pltpu.VMEM((D_MODEL, tm), F32), pltpu.VMEM((tm // ql, te, ql), F32),
                        pltpu.VMEM((tm // ql, te, ql), BF16)],
        compiler_params=pltpu.CompilerParams(dimension_semantics=("arbitrary", "arbitrary"),
                                             vmem_limit_bytes=VMEM_LIMIT),
        name="dense",
    )(u2t, down, upt, n, r2, e1, e2, h1, fn)


def _rope_lane_freqs():
    inv_freq = ROPE_THETA ** (-jnp.arange(ROT_HALF, dtype=F32) * (2.0 / ROT_DIM))
    lane = np.arange(LANES) % HEAD_DIM
    return jnp.where(lane < ROT_DIM, inv_freq[lane % ROT_HALF], 0.0).reshape(1, LANES).astype(F32)


def kernel(x, positions, norm_mix, w_in, conv_w, w_conv_out, w_attn_out, gate_bias, w_out,
           norm_ffn, peer_w_query, peer_sub_keys, peer_down, peer_up, final_norm):
    batch, seq, d = x.shape
    assert d == D_MODEL and w_in.shape[-1] == D_IN
    depth = w_in.shape[0]
    t = batch * seq
    h = x.reshape(t, d)
    pos2 = positions.reshape(t, 1)
    invf = _rope_lane_freqs()
    n_bcx = 3 * D_MODEL
    for layer in range(depth):
        w = w_in[layer]
        w_p = jnp.concatenate([w[:, :n_bcx], w[:, n_bcx + 3 * D_ATTN:],
                               w[:, n_bcx:n_bcx + 3 * D_ATTN]], axis=1).astype(BF16)
        yc, ga, qs, ks, vs = _proj_call(h, pos2, norm_mix[layer].reshape(1, d), invf, w_p,
                                        conv_w[layer], gate_bias[layer],
                                        w_conv_out[layer].astype(BF16), seq=seq)
        outs, lses = zip(*[_attn_call(qs[g], ks[g], vs[g], g, batch=batch, seq=seq)
                           for g in range(N_GROUPS)])
        h1 = _merge_call(h, yc, ga, outs, lses, gate_bias[layer],
                         w_attn_out[layer].astype(BF16), w_out[layer].astype(BF16))
        u2t, n, r2, e1, e2 = _route_call(
            h1, norm_ffn[layer].reshape(1, d), peer_w_query[layer].T.astype(BF16),
            peer_sub_keys[layer].astype(BF16))
        assert layer == depth - 1, "intermediate layers need a dense call without the final norm"
        h = _dense_call(u2t, peer_down[layer].astype(BF16), peer_up[layer].T.astype(BF16),
                        n, r2, e1, e2, h1, final_norm.reshape(1, d))
    return h.reshape(batch, seq, d)
```

```python
import functools

import numpy as np
import jax
import jax.numpy as jnp
from jax import lax
from jax.experimental import pallas as pl
from jax.experimental.pallas import tpu as pltpu

F32 = jnp.float32
BF16 = jnp.bfloat16

D_MODEL = 1024
HEAD_DIM = 64
HEADS_PER_GROUP = 8
DILATED_GROUPS = ((128, 1), (512, 4), (2048, 16))
N_GROUPS = len(DILATED_GROUPS)
D_GROUP = HEADS_PER_GROUP * HEAD_DIM
D_ATTN = N_GROUPS * D_GROUP
BLOCK = 128
ROT_DIM = HEAD_DIM // 4
ROT_HALF = ROT_DIM // 2
ROPE_THETA = 500000.0
PEER_HEADS = 8
PEER_N_KEYS = 128
PEER_TOPK = 16
PEER_D_HALF = 128
RMS_EPS = 1e-6
LANES = 128
SUBLANES = 8
BF16_ROWS = 2 * SUBLANES
RANK_CODE_BASE = 2.0 ** 120
VMEM_LIMIT = 56 * 1024 * 1024

COL_B, COL_C, COL_X, COL_GC, COL_GA, COL_QKV = 0, 1024, 2048, 3072, 4096, 5120
D_IN = COL_QKV + 3 * D_ATTN

TM_PROJ = 512
TM_MERGE = 512
TM_ROUTE = 512
TM_DENSE = 1024
TE_DENSE = SUBLANES * PEER_N_KEYS
QL_DENSE = 256
KEY_ROWS = 128


def _const_spec(shape):
    nd = len(shape)
    return pl.BlockSpec(shape, lambda *_: (0,) * nd, pipeline_mode=pl.Buffered(1))


def _rms(x, g):
    return x * lax.rsqrt(jnp.mean(x * x, axis=-1, keepdims=True) + RMS_EPS) * g


def _proj_kernel(x_ref, pos_ref, nm_ref, invf_ref, w_ref, cw_ref, gb_ref, wco_ref, *rest,
                 tm, seq):
    yc_ref, ga_ref = rest[0], rest[1]
    qkv_refs = rest[2:2 + 3 * N_GROUPS]
    zbuf_ref, t_ref = rest[2 + 3 * N_GROUPS:]
    i = pl.program_id(0)
    u = _rms(x_ref[...], nm_ref[...]).astype(BF16)

    def mm(c0, n):
        return jnp.dot(u, w_ref[:, c0:c0 + n], preferred_element_type=F32)

    @pl.when((i * tm) % seq == 0)
    def _():
        zbuf_ref[0:8, :] = jnp.zeros((8, D_MODEL), F32)

    z = mm(COL_C, D_MODEL) * mm(COL_X, D_MODEL)
    zbuf_ref[8:tm + 8, :] = z
    cw = cw_ref[...]
    conv = cw[0:1] * zbuf_ref[6:tm + 6, :] + cw[1:2] * zbuf_ref[7:tm + 7, :] + cw[2:3] * z
    zbuf_ref[0:8, :] = zbuf_ref[tm:tm + 8, :]
    yb = (mm(COL_B, D_MODEL) * conv).astype(BF16)
    yc = jnp.dot(yb, wco_ref[...], preferred_element_type=F32)
    yc_ref[...] = (jax.nn.sigmoid(mm(COL_GC, D_MODEL) + gb_ref[0:1, :]) * yc).astype(BF16)
    ga_ref[...] = mm(COL_GA, D_MODEL).astype(BF16)

    ang = pos_ref[...].astype(F32) * invf_ref[...]
    cosv = jnp.cos(ang)
    sinv = jnp.sin(ang)
    lane = lax.broadcasted_iota(jnp.int32, (1, LANES), 1) % HEAD_DIM
    sin_lo = jnp.where(lane < ROT_HALF, -sinv, 0.0)
    sin_hi = jnp.where((lane >= ROT_HALF) & (lane < ROT_DIM), sinv, 0.0)
    for blk in range(3 * N_GROUPS):
        t = mm(COL_QKV + blk * D_GROUP, D_GROUP)
        dil = DILATED_GROUPS[blk % N_GROUPS][1]
        out_ref = qkv_refs[blk]
        for s in range(D_GROUP // LANES):
            ts = t[:, s * LANES:(s + 1) * LANES]
            if blk < 2 * N_GROUPS:
                ts = (ts * cosv + pltpu.roll(ts, LANES - ROT_HALF, 1) * sin_lo
                      + pltpu.roll(ts, ROT_HALF, 1) * sin_hi)
            if dil == 1:
                out_ref[0, :, s * LANES:(s + 1) * LANES] = ts.astype(BF16)
            else:
                t_ref[s] = ts
        if dil > 1:
            for r in range(dil):
                for s in range(D_GROUP // LANES):
                    out_ref[r, :, s * LANES:(s + 1) * LANES] = t_ref[
                        s, pl.ds(r, tm // dil, stride=dil), :].astype(BF16)


def _proj_call(x2, pos2, nm, invf, w_p, cw, gb, wco, *, seq):
    t = x2.shape[0]
    tm = TM_PROJ
    row = lambda n: pl.BlockSpec((tm, n), lambda i: (i, 0))
    dils = [d for _, d in DILATED_GROUPS] * 3
    qkv_specs = [pl.BlockSpec((d, tm // d, D_GROUP), lambda i: (0, i, 0)) for d in dils]
    qkv_shapes = [jax.ShapeDtypeStruct((d, t // d, D_GROUP), BF16) for d in dils]
    res = pl.pallas_call(
        functools.partial(_proj_kernel, tm=tm, seq=seq),
        grid=(t // tm,),
        in_specs=[row(D_MODEL), row(1), _const_spec((1, D_MODEL)), _const_spec((1, LANES)),
                  _const_spec((D_MODEL, D_IN)), _const_spec((3, D_MODEL)),
                  _const_spec((2, D_MODEL)), _const_spec((D_MODEL, D_MODEL))],
        out_specs=[row(D_MODEL), row(D_MODEL)] + qkv_specs,
        out_shape=[jax.ShapeDtypeStruct((t, D_MODEL), BF16),
                   jax.ShapeDtypeStruct((t, D_MODEL), BF16)] + qkv_shapes,
        scratch_shapes=[pltpu.VMEM((tm + 8, D_MODEL), F32),
                        pltpu.VMEM((D_GROUP // LANES, tm, LANES), F32)],
        compiler_params=pltpu.CompilerParams(dimension_semantics=("arbitrary",),
                                             vmem_limit_bytes=VMEM_LIMIT),
        name="proj",
    )(x2, pos2, nm, invf, w_p, cw, gb, wco)
    yc, ga = res[0], res[1]
    qs, ks, vs = res[2:2 + N_GROUPS], res[2 + N_GROUPS:2 + 2 * N_GROUPS], res[2 + 2 * N_GROUPS:]
    return yc, ga, qs, ks, vs


def _attn_kernel(q_ref, kp_ref, kc_ref, vp_ref, vc_ref, o_ref, lse_ref, *, steps):
    n = pl.program_id(2)
    qi = lax.broadcasted_iota(jnp.int32, (BLOCK, 2 * BLOCK), 0)
    ki = lax.broadcasted_iota(jnp.int32, (BLOCK, 2 * BLOCK), 1)
    dist = BLOCK + qi - ki
    valid = (dist >= 0) & (dist <= steps) & ((n > 0) | (ki >= BLOCK))
    first = lax.broadcasted_iota(jnp.int32, (1, LANES), 1) < HEAD_DIM
    ones = jnp.ones((2 * BLOCK, LANES), BF16)
    scale = HEAD_DIM ** -0.5
    n_slab = D_GROUP // LANES
    slabs = [slice(hp * LANES, (hp + 1) * LANES) for hp in range(n_slab)]
    picks = (first, jnp.logical_not(first))
    scores = []
    for sl in slabs:
        q2 = q_ref[:, sl] * scale
        k2 = jnp.concatenate([kp_ref[:, sl], kc_ref[:, sl]], axis=0)
        for pick in picks:
            qh = jnp.where(pick, q2, jnp.zeros((), BF16))
            s = lax.dot_general(qh, k2, (((1,), (1,)), ((), ())), preferred_element_type=F32)
            scores.append(jnp.where(valid, s, -jnp.inf))
    maxes = [jnp.max(s, axis=-1, keepdims=True) for s in scores]
    probs = [jnp.exp(s - m).astype(BF16) for s, m in zip(scores, maxes)]
    for hp, sl in enumerate(slabs):
        v2 = jnp.concatenate([vp_ref[:, sl], vc_ref[:, sl]], axis=0)
        pa, pb = probs[2 * hp], probs[2 * hp + 1]
        l = jnp.where(first, jnp.dot(pa, ones, preferred_element_type=F32),
                      jnp.dot(pb, ones, preferred_element_type=F32))
        o = jnp.where(first, jnp.dot(pa, v2, preferred_element_type=F32),
                      jnp.dot(pb, v2, preferred_element_type=F32))
        o_ref[:, sl] = o / l
        lse_ref[:, sl] = jnp.where(first, maxes[2 * hp], maxes[2 * hp + 1]) + jnp.log(l)


def _attn_call(q, k, v, g, *, batch, seq):
    window, dil = DILATED_GROUPS[g]
    steps = window // dil
    assert steps <= BLOCK and seq % (dil * BLOCK) == 0
    nb = seq // (dil * BLOCK)

    def spec(prev):
        def imap(b, r, n):
            nn = jnp.maximum(n - 1, 0) if prev else n
            return (r, b * nb + nn, 0)
        return pl.BlockSpec((None, BLOCK, D_GROUP), imap)

    return pl.pallas_call(
        functools.partial(_attn_kernel, steps=steps),
        grid=(batch, dil, nb),
        in_specs=[spec(False), spec(True), spec(False), spec(True), spec(False)],
        out_specs=[spec(False), spec(False)],
        out_shape=[jax.ShapeDtypeStruct(q.shape, F32)] * 2,
        compiler_params=pltpu.CompilerParams(
            dimension_semantics=("arbitrary", "arbitrary", "arbitrary")),
        name=f"attn{g}",
    )(q, k, k, v, v)


def _merge_kernel(x_ref, yc_ref, ga_ref, o0_ref, o1_ref, o2_ref, l0_ref, l1_ref, l2_ref,
                  gb_ref, wao_ref, wo_ref, h_ref, *scratch, tm):
    def natural(ref, g, scr):
        dil = DILATED_GROUPS[g][1]
        if dil == 1:
            return ref[0]
        n_slab = D_GROUP // LANES
        for r in range(dil):
            for s in range(n_slab):
                scr[s, pl.ds(r, tm // dil, stride=dil), :] = ref[r, :, s * LANES:(s + 1) * LANES]
        return jnp.concatenate([scr[s] for s in range(n_slab)], axis=1)

    scr = iter(scratch)
    o_refs, l_refs = (o0_ref, o1_ref, o2_ref), (l0_ref, l1_ref, l2_ref)
    ls = [natural(l_refs[g], g, next(scr) if DILATED_GROUPS[g][1] > 1 else None)
          for g in range(N_GROUPS)]
    os_ = [natural(o_refs[g], g, next(scr) if DILATED_GROUPS[g][1] > 1 else None)
           for g in range(N_GROUPS)]
    mx = functools.reduce(jnp.maximum, ls)
    es = [jnp.exp(l - mx) for l in ls]
    o = sum(e * ov for e, ov in zip(es, os_)) / sum(es)
    ya = jnp.dot(o.astype(BF16), wao_ref[...], preferred_element_type=F32)
    merged = (yc_ref[...].astype(F32)
              + jax.nn.sigmoid(ga_ref[...].astype(F32) + gb_ref[1:2, :]) * ya)
    h_ref[...] = x_ref[...] + jnp.dot(merged.astype(BF16), wo_ref[...],
                                      preferred_element_type=F32)


def _merge_call(x2, yc, ga, outs, lses, gb, wao, wo):
    t = x2.shape[0]
    tm = TM_MERGE
    row = lambda n: pl.BlockSpec((tm, n), lambda i: (i, 0))
    dils = [d for _, d in DILATED_GROUPS]
    grp = [pl.BlockSpec((d, tm // d, D_GROUP), lambda i: (0, i, 0)) for d in dils]
    n_scr = 2 * sum(d > 1 for d in dils)
    return pl.pallas_call(
        functools.partial(_merge_kernel, tm=tm),
        grid=(t // tm,),
        in_specs=[row(D_MODEL), row(D_MODEL), row(D_MODEL)] + grp + grp
                 + [_const_spec((2, D_MODEL)), _const_spec((D_GROUP, D_MODEL)),
                    _const_spec((D_MODEL, D_MODEL))],
        out_specs=row(D_MODEL),
        out_shape=jax.ShapeDtypeStruct((t, D_MODEL), F32),
        scratch_shapes=[pltpu.VMEM((D_GROUP // LANES, tm, LANES), F32)] * n_scr,
        compiler_params=pltpu.CompilerParams(dimension_semantics=("arbitrary",),
                                             vmem_limit_bytes=VMEM_LIMIT),
        name="merge",
    )(x2, yc, ga, *outs, *lses, gb, wao, wo)


def _pair_candidates():
    return [(a, b) for a in range(PEER_TOPK) for b in range(PEER_TOPK)
            if (a + 1) * (b + 1) <= PEER_TOPK]


def _rank_code(r):
    return -RANK_CODE_BASE * (1.0 + r / 32.0)


def _route_kernel(h_ref, nf_ref, wqt_ref, sk_ref,
                  u2q_ref, n_ref, r2_ref, e1_ref, e2_ref,
                  qt_ref, ts_ref, s_ref, rk_ref, *, tm, ql):
    u2 = _rms(h_ref[...], nf_ref[...])
    u2t = u2.T.astype(BF16)
    for qq in range(tm // ql):
        u2q_ref[qq] = u2t[:, qq * ql:(qq + 1) * ql]
    qt_ref[...] = jnp.dot(wqt_ref[...], u2t, preferred_element_type=F32).astype(BF16)
    n_lb = tm // LANES

    def head_body(h, carry):
        for c in range(2):
            r0 = pl.multiple_of(h * (2 * PEER_D_HALF) + c * PEER_D_HALF, PEER_D_HALF)
            s_ref[c, h] = jnp.dot(sk_ref[h, c], qt_ref[pl.ds(r0, PEER_D_HALF), :],
                                  preferred_element_type=F32)
            for lb in range(n_lb):
                ln = slice(lb * LANES, (lb + 1) * LANES)
                cur = s_ref[c, h, :, ln]
                for r in range(PEER_TOPK):
                    m = jnp.max(cur, axis=0, keepdims=True)
                    ts_ref[c, r, lb, pl.ds(h, 1), :] = m
                    cur = jnp.where(cur >= m, _rank_code(r), cur)
                rk_ref[c, h, :, ln] = cur
        return carry

    lax.fori_loop(0, PEER_HEADS, head_body, 0)

    def top_rows(c, r):
        return jnp.concatenate([ts_ref[c, r, lb] for lb in range(n_lb)], axis=1)

    top1 = [top_rows(0, r) for r in range(PEER_TOPK)]
    top2 = [top_rows(1, r) for r in range(PEER_TOPK)]
    pairs = _pair_candidates()
    cands = [top1[a] + top2[b] for a, b in pairs]
    cmax = top1[0] + top2[0]
    cur = list(cands)
    for r in range(PEER_TOPK):
        m = functools.reduce(jnp.maximum, cur)
        if r + 1 < PEER_TOPK:
            cur = [jnp.where(c >= m, -jnp.inf, c) for c in cur]
    tau = m
    sel = [c >= tau for c in cands]
    z = functools.reduce(jnp.add, [jnp.where(s, jnp.exp(c - cmax), 0.0)
                                   for s, c in zip(sel, cands)])
    inv_z = 1.0 / z
    cnt = []
    for a in range(PEER_TOPK):
        cnt.append(functools.reduce(
            jnp.add, [jnp.where(s, 1.0, 0.0) for s, (pa, _) in zip(sel, pairs) if pa == a]))

    for h in range(PEER_HEADS):
        row = slice(h, h + 1)
        for lb in range(n_lb):
            ln = slice(lb * LANES, (lb + 1) * LANES)
            code1 = rk_ref[0, h, :, ln]
            n = jnp.zeros(code1.shape, F32)
            for a in range(PEER_TOPK):
                n = jnp.where(code1 == _rank_code(a), cnt[a][row, ln], n)
            n_ref[h, :, ln] = n
            e1_ref[h, :, ln] = jnp.exp(s_ref[0, h, :, ln] - top1[0][row, ln]) * inv_z[row, ln]
            code2 = rk_ref[1, h, :, ln]
            rank2 = jnp.where(code2 <= -RANK_CODE_BASE,
                              (code2 * (-1.0 / RANK_CODE_BASE) - 1.0) * 32.0, float(PEER_TOPK))
            r2_ref[h, :, ln] = rank2.astype(BF16)
            e2_ref[h, :, ln] = jnp.exp(s_ref[1, h, :, ln] - top2[0][row, ln]).astype(BF16)


def _route_call(h1, nf, wqt, sk):
    t = h1.shape[0]
    tm, ql = TM_ROUTE, QL_DENSE
    big = pl.BlockSpec((PEER_HEADS, PEER_N_KEYS, tm), lambda i: (0, 0, i))
    shape = lambda dt: jax.ShapeDtypeStruct((PEER_HEADS, PEER_N_KEYS, t), dt)
    return pl.pallas_call(
        functools.partial(_route_kernel, tm=tm, ql=ql),
        grid=(t // tm,),
        in_specs=[pl.BlockSpec((tm, D_MODEL), lambda i: (i, 0)), _const_spec((1, D_MODEL)),
                  _const_spec(wqt.shape), _const_spec(sk.shape)],
        out_specs=[pl.BlockSpec((tm // ql, D_MODEL, ql), lambda i: (i, 0, 0)),
                   big, big, big, big],
        out_shape=[jax.ShapeDtypeStruct((t // ql, D_MODEL, ql), BF16), shape(F32), shape(BF16),
                   shape(F32), shape(BF16)],
        scratch_shapes=[pltpu.VMEM((wqt.shape[0], tm), BF16),
                        pltpu.VMEM((2, PEER_TOPK, tm // LANES, PEER_HEADS, LANES), F32),
                        pltpu.VMEM((2, PEER_HEADS, PEER_N_KEYS, tm), F32),
                        pltpu.VMEM((2, PEER_HEADS, PEER_N_KEYS, tm), F32)],
        compiler_params=pltpu.CompilerParams(dimension_semantics=("arbitrary",),
                                             vmem_limit_bytes=VMEM_LIMIT),
        name="route",
    )(h1, nf, wqt, sk)


def _dense_kernel(u2q_ref, down_ref, upt_ref, n_ref, r2_ref, e1_ref, e2_ref,
                  h_ref, fn_ref, out_ref, acc_ref, ht_ref, wt_ref, *, tm, te, ql):
    j = pl.program_id(1)
    nc = te // PEER_N_KEYS
    nq = tm // ql

    @pl.when(j == 0)
    def _():
        acc_ref[...] = jnp.zeros_like(acc_ref)

    def hidden(q):
        ht_ref[q] = jnp.dot(down_ref[...], u2q_ref[q], preferred_element_type=F32)

    def key_rows(ref, h, cc, ln):
        return ref[h, cc:cc + 1, ln].astype(BF16)

    hidden(0)
    for q in range(nq):
        if q + 1 < nq:
            hidden(q + 1)
        ln = slice(q * ql, (q + 1) * ql)
        for cc in range(nc):
            for kb in range(PEER_N_KEYS // KEY_ROWS):
                keys = slice(kb * KEY_ROWS, (kb + 1) * KEY_ROWS)
                rows = slice(cc * PEER_N_KEYS + kb * KEY_ROWS,
                             cc * PEER_N_KEYS + (kb + 1) * KEY_ROWS)
                g = jnp.zeros((KEY_ROWS, ql), BF16)
                for h in range(PEER_HEADS):
                    gate = e2_ref[h, keys, ln] * key_rows(e1_ref, h, cc, ln)
                    g = g + jnp.where(r2_ref[h, keys, ln] < key_rows(n_ref, h, cc, ln), gate,
                                      jnp.zeros((), BF16))
                hv = ht_ref[q, rows, :]
                act = 0.5 * hv * (1.0 + lax.erf(hv * np.float32(2.0 ** -0.5)))
                wt_ref[q, rows, :] = g * act.astype(BF16)
        acc_ref[:, q * ql:(q + 1) * ql] += jnp.dot(upt_ref[...], wt_ref[q],
                                                   preferred_element_type=F32)

    @pl.when(j == pl.num_programs(1) - 1)
    def _():
        out_ref[...] = _rms(h_ref[...] + acc_ref[...].T, fn_ref[...])


def _dense_call(u2q, down, upt, n, r2, e1, e2, h1, fn):
    t = h1.shape[0]
    n_exp = down.shape[0]
    tm, te, ql = TM_DENSE, TE_DENSE, QL_DENSE
    nc = te // PEER_N_KEYS
    big = pl.BlockSpec((PEER_HEADS, PEER_N_KEYS, tm), lambda i, j: (0, 0, i))
    rows = pl.BlockSpec((PEER_HEADS, nc, tm), lambda i, j: (0, j, i))
    return pl.pallas_call(
        functools.partial(_dense_kernel, tm=tm, te=te, ql=ql),
        grid=(t // tm, n_exp // te),
        in_specs=[pl.BlockSpec((tm // ql, D_MODEL, ql), lambda i, j: (i, 0, 0)),
                  pl.BlockSpec((te, D_MODEL), lambda i, j: (j, 0)),
                  pl.BlockSpec((D_MODEL, te), lambda i, j: (0, j)),
                  rows, big, rows, big,
                  pl.BlockSpec((tm, D_MODEL), lambda i, j: (i, 0)),
                  pl.BlockSpec((1, D_MODEL), lambda i, j: (0, 0))],
        out_specs=pl.BlockSpec((tm, D_MODEL), lambda i, j: (i, 0)),
        out_shape=jax.ShapeDtypeStruct((t, D_MODEL), F32),
        scratch_shapes=[pltpu.VMEM((D_MODEL, tm), F32), pltpu.VMEM((tm // ql, te, ql), F32),
                        pltpu.VMEM((tm // ql, te, ql), BF16)],
        compiler_params=pltpu.CompilerParams(dimension_semantics=("arbitrary", "arbitrary"),
                                             vmem_limit_bytes=VMEM_LIMIT),
        name="dense",
    )(u2q, down, upt, n, r2, e1, e2, h1, fn)


def _rope_lane_freqs():
    inv_freq = ROPE_THETA ** (-jnp.arange(ROT_HALF, dtype=F32) * (2.0 / ROT_DIM))
    lane = np.arange(LANES) % HEAD_DIM
    return jnp.where(lane < ROT_DIM, inv_freq[lane % ROT_HALF], 0.0).reshape(1, LANES).astype(F32)


def kernel(x, positions, norm_mix, w_in, conv_w, w_conv_out, w_attn_out, gate_bias, w_out,
           norm_ffn, peer_w_query, peer_sub_keys, peer_down, peer_up, final_norm):
    batch, seq, d = x.shape
    assert d == D_MODEL and w_in.shape[-1] == D_IN
    depth = w_in.shape[0]
    t = batch * seq
    h = x.reshape(t, d)
    pos2 = positions.reshape(t, 1)
    invf = _rope_lane_freqs()
    n_bcx = 3 * D_MODEL
    for layer in range(depth):
        w = w_in[layer]
        w_p = jnp.concatenate([w[:, :n_bcx], w[:, n_bcx + 3 * D_ATTN:],
                               w[:, n_bcx:n_bcx + 3 * D_ATTN]], axis=1).astype(BF16)
        yc, ga, qs, ks, vs = _proj_call(h, pos2, norm_mix[layer].reshape(1, d), invf, w_p,
                                        conv_w[layer], gate_bias[layer],
                                        w_conv_out[layer].astype(BF16), seq=seq)
        outs, lses = zip(*[_attn_call(qs[g], ks[g], vs[g], g, batch=batch, seq=seq)
                           for g in range(N_GROUPS)])
        h1 = _merge_call(h, yc, ga, outs, lses, gate_bias[layer],
                         w_attn_out[layer].astype(BF16), w_out[layer].astype(BF16))
        u2q, n, r2, e1, e2 = _route_call(
            h1, norm_ffn[layer].reshape(1, d), peer_w_query[layer].T.astype(BF16),
            peer_sub_keys[layer].astype(BF16))
        assert layer == depth - 1, "intermediate layers need a dense call without the final norm"
        h = _dense_call(u2q, peer_down[layer].astype(BF16), peer_up[layer].T.astype(BF16),
                        n, r2, e1, e2, h1, final_norm.reshape(1, d))
    return h.reshape(batch, seq, d)
```

```python
import functools

import numpy as np
import jax
import jax.numpy as jnp
from jax import lax
from jax.experimental import pallas as pl
from jax.experimental.pallas import tpu as pltpu

F32 = jnp.float32
BF16 = jnp.bfloat16

D_MODEL = 1024
HEAD_DIM = 64
HEADS_PER_GROUP = 8
DILATED_GROUPS = ((128, 1), (512, 4), (2048, 16))
N_GROUPS = len(DILATED_GROUPS)
D_GROUP = HEADS_PER_GROUP * HEAD_DIM
D_ATTN = N_GROUPS * D_GROUP
BLOCK = 128
ROT_DIM = HEAD_DIM // 4
ROT_HALF = ROT_DIM // 2
ROPE_THETA = 500000.0
PEER_HEADS = 8
PEER_N_KEYS = 128
PEER_TOPK = 16
PEER_D_HALF = 128
RMS_EPS = 1e-6
LANES = 128
SUBLANES = 8
BF16_ROWS = 2 * SUBLANES
RANK_CODE_BASE = 2.0 ** 120
VMEM_LIMIT = 60 * 1024 * 1024

COL_B, COL_C, COL_X, COL_GC, COL_GA, COL_QKV = 0, 1024, 2048, 3072, 4096, 5120
D_IN = COL_QKV + 3 * D_ATTN

TM_PROJ = 512
TM_MERGE = 512
TM_ROUTE = 512
TM_DENSE = 1024
TE_DENSE = SUBLANES * PEER_N_KEYS
QL_DENSE = 256


def _const_spec(shape):
    nd = len(shape)
    return pl.BlockSpec(shape, lambda *_: (0,) * nd, pipeline_mode=pl.Buffered(1))


def _rms(x, g):
    return x * lax.rsqrt(jnp.mean(x * x, axis=-1, keepdims=True) + RMS_EPS) * g


def _proj_kernel(x_ref, pos_ref, nm_ref, invf_ref, w_ref, cw_ref, gb_ref, wco_ref, *rest,
                 tm, seq):
    yc_ref, ga_ref = rest[0], rest[1]
    qkv_refs = rest[2:2 + 3 * N_GROUPS]
    zbuf_ref, t_ref = rest[2 + 3 * N_GROUPS:]
    i = pl.program_id(0)
    u = _rms(x_ref[...], nm_ref[...]).astype(BF16)

    def mm(c0, n):
        return jnp.dot(u, w_ref[:, c0:c0 + n], preferred_element_type=F32)

    @pl.when((i * tm) % seq == 0)
    def _():
        zbuf_ref[0:8, :] = jnp.zeros((8, D_MODEL), F32)

    z = mm(COL_C, D_MODEL) * mm(COL_X, D_MODEL)
    zbuf_ref[8:tm + 8, :] = z
    cw = cw_ref[...]
    conv = cw[0:1] * zbuf_ref[6:tm + 6, :] + cw[1:2] * zbuf_ref[7:tm + 7, :] + cw[2:3] * z
    zbuf_ref[0:8, :] = zbuf_ref[tm:tm + 8, :]
    yb = (mm(COL_B, D_MODEL) * conv).astype(BF16)
    yc = jnp.dot(yb, wco_ref[...], preferred_element_type=F32)
    yc_ref[...] = (jax.nn.sigmoid(mm(COL_GC, D_MODEL) + gb_ref[0:1, :]) * yc).astype(BF16)
    ga_ref[...] = mm(COL_GA, D_MODEL).astype(BF16)

    ang = pos_ref[...].astype(F32) * invf_ref[...]
    cosv = jnp.cos(ang)
    sinv = jnp.sin(ang)
    lane = lax.broadcasted_iota(jnp.int32, (1, LANES), 1) % HEAD_DIM
    sin_lo = jnp.where(lane < ROT_HALF, -sinv, 0.0)
    sin_hi = jnp.where((lane >= ROT_HALF) & (lane < ROT_DIM), sinv, 0.0)
    for blk in range(3 * N_GROUPS):
        t = mm(COL_QKV + blk * D_GROUP, D_GROUP)
        dil = DILATED_GROUPS[blk % N_GROUPS][1]
        out_ref = qkv_refs[blk]
        for s in range(D_GROUP // LANES):
            ts = t[:, s * LANES:(s + 1) * LANES]
            if blk < 2 * N_GROUPS:
                ts = (ts * cosv + pltpu.roll(ts, LANES - ROT_HALF, 1) * sin_lo
                      + pltpu.roll(ts, ROT_HALF, 1) * sin_hi)
            if dil == 1:
                out_ref[0, :, s * LANES:(s + 1) * LANES] = ts.astype(BF16)
            else:
                t_ref[s] = ts
        if dil > 1:
            for r in range(dil):
                for s in range(D_GROUP // LANES):
                    out_ref[r, :, s * LANES:(s + 1) * LANES] = t_ref[
                        s, pl.ds(r, tm // dil, stride=dil), :].astype(BF16)


def _proj_call(x2, pos2, nm, invf, w_p, cw, gb, wco, *, seq):
    t = x2.shape[0]
    tm = TM_PROJ
    row = lambda n: pl.BlockSpec((tm, n), lambda i: (i, 0))
    dils = [d for _, d in DILATED_GROUPS] * 3
    qkv_specs = [pl.BlockSpec((d, tm // d, D_GROUP), lambda i: (0, i, 0)) for d in dils]
    qkv_shapes = [jax.ShapeDtypeStruct((d, t // d, D_GROUP), BF16) for d in dils]
    res = pl.pallas_call(
        functools.partial(_proj_kernel, tm=tm, seq=seq),
        grid=(t // tm,),
        in_specs=[row(D_MODEL), row(1), _const_spec((1, D_MODEL)), _const_spec((1, LANES)),
                  _const_spec((D_MODEL, D_IN)), _const_spec((3, D_MODEL)),
                  _const_spec((2, D_MODEL)), _const_spec((D_MODEL, D_MODEL))],
        out_specs=[row(D_MODEL), row(D_MODEL)] + qkv_specs,
        out_shape=[jax.ShapeDtypeStruct((t, D_MODEL), BF16),
                   jax.ShapeDtypeStruct((t, D_MODEL), BF16)] + qkv_shapes,
        scratch_shapes=[pltpu.VMEM((tm + 8, D_MODEL), F32),
                        pltpu.VMEM((D_GROUP // LANES, tm, LANES), F32)],
        compiler_params=pltpu.CompilerParams(dimension_semantics=("arbitrary",),
                                             vmem_limit_bytes=VMEM_LIMIT),
        name="proj",
    )(x2, pos2, nm, invf, w_p, cw, gb, wco)
    yc, ga = res[0], res[1]
    qs, ks, vs = res[2:2 + N_GROUPS], res[2 + N_GROUPS:2 + 2 * N_GROUPS], res[2 + 2 * N_GROUPS:]
    return yc, ga, qs, ks, vs


def _attn_kernel(q_ref, kp_ref, kc_ref, vp_ref, vc_ref, o_ref, lse_ref, *, steps):
    n = pl.program_id(2)
    qi = lax.broadcasted_iota(jnp.int32, (BLOCK, 2 * BLOCK), 0)
    ki = lax.broadcasted_iota(jnp.int32, (BLOCK, 2 * BLOCK), 1)
    dist = BLOCK + qi - ki
    valid = (dist >= 0) & (dist <= steps) & ((n > 0) | (ki >= BLOCK))
    first = lax.broadcasted_iota(jnp.int32, (1, LANES), 1) < HEAD_DIM
    ones = jnp.ones((2 * BLOCK, LANES), BF16)
    scale = HEAD_DIM ** -0.5
    n_slab = D_GROUP // LANES
    slabs = [slice(hp * LANES, (hp + 1) * LANES) for hp in range(n_slab)]
    picks = (first, jnp.logical_not(first))
    scores = []
    for sl in slabs:
        q2 = q_ref[:, sl] * scale
        k2 = jnp.concatenate([kp_ref[:, sl], kc_ref[:, sl]], axis=0)
        for pick in picks:
            qh = jnp.where(pick, q2, jnp.zeros((), BF16))
            s = lax.dot_general(qh, k2, (((1,), (1,)), ((), ())), preferred_element_type=F32)
            scores.append(jnp.where(valid, s, -jnp.inf))
    maxes = [jnp.max(s, axis=-1, keepdims=True) for s in scores]
    probs = [jnp.exp(s - m).astype(BF16) for s, m in zip(scores, maxes)]
    for hp, sl in enumerate(slabs):
        v2 = jnp.concatenate([vp_ref[:, sl], vc_ref[:, sl]], axis=0)
        pa, pb = probs[2 * hp], probs[2 * hp + 1]
        l = jnp.where(first, jnp.dot(pa, ones, preferred_element_type=F32),
                      jnp.dot(pb, ones, preferred_element_type=F32))
        o = jnp.where(first, jnp.dot(pa, v2, preferred_element_type=F32),
                      jnp.dot(pb, v2, preferred_element_type=F32))
        o_ref[:, sl] = o / l
        lse_ref[:, sl] = jnp.where(first, maxes[2 * hp], maxes[2 * hp + 1]) + jnp.log(l)


def _attn_call(q, k, v, g, *, batch, seq):
    window, dil = DILATED_GROUPS[g]
    steps = window // dil
    assert steps <= BLOCK and seq % (dil * BLOCK) == 0
    nb = seq // (dil * BLOCK)

    def spec(prev):
        def imap(b, r, n):
            nn = jnp.maximum(n - 1, 0) if prev else n
            return (r, b * nb + nn, 0)
        return pl.BlockSpec((None, BLOCK, D_GROUP), imap)

    return pl.pallas_call(
        functools.partial(_attn_kernel, steps=steps),
        grid=(batch, dil, nb),
        in_specs=[spec(False), spec(True), spec(False), spec(True), spec(False)],
        out_specs=[spec(False), spec(False)],
        out_shape=[jax.ShapeDtypeStruct(q.shape, F32)] * 2,
        compiler_params=pltpu.CompilerParams(
            dimension_semantics=("arbitrary", "arbitrary", "arbitrary")),
        name=f"attn{g}",
    )(q, k, k, v, v)


def _merge_kernel(x_ref, yc_ref, ga_ref, o0_ref, o1_ref, o2_ref, l0_ref, l1_ref, l2_ref,
                  gb_ref, wao_ref, wo_ref, h_ref, *scratch, tm):
    def natural(ref, g, scr):
        dil = DILATED_GROUPS[g][1]
        if dil == 1:
            return ref[0]
        n_slab = D_GROUP // LANES
        for r in range(dil):
            for s in range(n_slab):
                scr[s, pl.ds(r, tm // dil, stride=dil), :] = ref[r, :, s * LANES:(s + 1) * LANES]
        return jnp.concatenate([scr[s] for s in range(n_slab)], axis=1)

    scr = iter(scratch)
    o_refs, l_refs = (o0_ref, o1_ref, o2_ref), (l0_ref, l1_ref, l2_ref)
    ls = [natural(l_refs[g], g, next(scr) if DILATED_GROUPS[g][1] > 1 else None)
          for g in range(N_GROUPS)]
    os_ = [natural(o_refs[g], g, next(scr) if DILATED_GROUPS[g][1] > 1 else None)
           for g in range(N_GROUPS)]
    mx = functools.reduce(jnp.maximum, ls)
    es = [jnp.exp(l - mx) for l in ls]
    o = sum(e * ov for e, ov in zip(es, os_)) / sum(es)
    ya = jnp.dot(o.astype(BF16), wao_ref[...], preferred_element_type=F32)
    merged = (yc_ref[...].astype(F32)
              + jax.nn.sigmoid(ga_ref[...].astype(F32) + gb_ref[1:2, :]) * ya)
    h_ref[...] = x_ref[...] + jnp.dot(merged.astype(BF16), wo_ref[...],
                                      preferred_element_type=F32)


def _merge_call(x2, yc, ga, outs, lses, gb, wao, wo):
    t = x2.shape[0]
    tm = TM_MERGE
    row = lambda n: pl.BlockSpec((tm, n), lambda i: (i, 0))
    dils = [d for _, d in DILATED_GROUPS]
    grp = [pl.BlockSpec((d, tm // d, D_GROUP), lambda i: (0, i, 0)) for d in dils]
    n_scr = 2 * sum(d > 1 for d in dils)
    return pl.pallas_call(
        functools.partial(_merge_kernel, tm=tm),
        grid=(t // tm,),
        in_specs=[row(D_MODEL), row(D_MODEL), row(D_MODEL)] + grp + grp
                 + [_const_spec((2, D_MODEL)), _const_spec((D_GROUP, D_MODEL)),
                    _const_spec((D_MODEL, D_MODEL))],
        out_specs=row(D_MODEL),
        out_shape=jax.ShapeDtypeStruct((t, D_MODEL), F32),
        scratch_shapes=[pltpu.VMEM((D_GROUP // LANES, tm, LANES), F32)] * n_scr,
        compiler_params=pltpu.CompilerParams(dimension_semantics=("arbitrary",),
                                             vmem_limit_bytes=VMEM_LIMIT),
        name="merge",
    )(x2, yc, ga, *outs, *lses, gb, wao, wo)


def _pair_candidates():
    return [(a, b) for a in range(PEER_TOPK) for b in range(PEER_TOPK)
            if (a + 1) * (b + 1) <= PEER_TOPK]


def _rank_code(r):
    return -RANK_CODE_BASE * (1.0 + r / 32.0)


def _route_kernel(h_ref, nf_ref, wqt_ref, sk_ref,
                  u2q_ref, n_ref, r2_ref, e1_ref, e2_ref,
                  qt_ref, ts_ref, s_ref, rk_ref, *, tm, ql):
    u2 = _rms(h_ref[...], nf_ref[...])
    u2t = u2.T.astype(BF16)
    for qq in range(tm // ql):
        u2q_ref[qq] = u2t[:, qq * ql:(qq + 1) * ql]
    qt_ref[...] = jnp.dot(wqt_ref[...], u2t, preferred_element_type=F32).astype(BF16)
    n_lb = tm // LANES

    def head_body(h, carry):
        for c in range(2):
            r0 = pl.multiple_of(h * (2 * PEER_D_HALF) + c * PEER_D_HALF, PEER_D_HALF)
            s_ref[c, h] = jnp.dot(sk_ref[h, c], qt_ref[pl.ds(r0, PEER_D_HALF), :],
                                  preferred_element_type=F32)
            for lb in range(n_lb):
                ln = slice(lb * LANES, (lb + 1) * LANES)
                cur = s_ref[c, h, :, ln]
                for r in range(PEER_TOPK):
                    m = jnp.max(cur, axis=0, keepdims=True)
                    ts_ref[c, r, lb, pl.ds(h, 1), :] = m
                    cur = jnp.where(cur >= m, _rank_code(r), cur)
                rk_ref[c, h, :, ln] = cur
        return carry

    lax.fori_loop(0, PEER_HEADS, head_body, 0)

    def top_rows(c, r):
        return jnp.concatenate([ts_ref[c, r, lb] for lb in range(n_lb)], axis=1)

    top1 = [top_rows(0, r) for r in range(PEER_TOPK)]
    top2 = [top_rows(1, r) for r in range(PEER_TOPK)]
    pairs = _pair_candidates()
    cands = [top1[a] + top2[b] for a, b in pairs]
    cmax = top1[0] + top2[0]
    cur = list(cands)
    for r in range(PEER_TOPK):
        m = functools.reduce(jnp.maximum, cur)
        if r + 1 < PEER_TOPK:
            cur = [jnp.where(c >= m, -jnp.inf, c) for c in cur]
    tau = m
    sel = [c >= tau for c in cands]
    z = functools.reduce(jnp.add, [jnp.where(s, jnp.exp(c - cmax), 0.0)
                                   for s, c in zip(sel, cands)])
    half_inv_z = 0.5 / z
    cnt = []
    for a in range(PEER_TOPK):
        cnt.append(functools.reduce(
            jnp.add, [jnp.where(s, 1.0, 0.0) for s, (pa, _) in zip(sel, pairs) if pa == a]))

    for h in range(PEER_HEADS):
        row = slice(h, h + 1)
        for lb in range(n_lb):
            ln = slice(lb * LANES, (lb + 1) * LANES)
            code1 = rk_ref[0, h, :, ln]
            n = jnp.zeros(code1.shape, F32)
            for a in range(PEER_TOPK):
                n = jnp.where(code1 == _rank_code(a), cnt[a][row, ln], n)
            n_ref[h, :, ln] = n
            e1_ref[h, :, ln] = jnp.exp(s_ref[0, h, :, ln] - top1[0][row, ln]) * half_inv_z[row, ln]
            code2 = rk_ref[1, h, :, ln]
            rank2 = jnp.where(code2 <= -RANK_CODE_BASE,
                              (code2 * (-1.0 / RANK_CODE_BASE) - 1.0) * 32.0, float(PEER_TOPK))
            r2_ref[h, :, ln] = rank2.astype(BF16)
            e2_ref[h, :, ln] = jnp.exp(s_ref[1, h, :, ln] - top2[0][row, ln]).astype(BF16)


def _route_call(h1, nf, wqt, sk):
    t = h1.shape[0]
    tm, ql = TM_ROUTE, QL_DENSE
    big = pl.BlockSpec((PEER_HEADS, PEER_N_KEYS, tm), lambda i: (0, 0, i))
    shape = lambda dt: jax.ShapeDtypeStruct((PEER_HEADS, PEER_N_KEYS, t), dt)
    return pl.pallas_call(
        functools.partial(_route_kernel, tm=tm, ql=ql),
        grid=(t // tm,),
        in_specs=[pl.BlockSpec((tm, D_MODEL), lambda i: (i, 0)), _const_spec((1, D_MODEL)),
                  _const_spec(wqt.shape), _const_spec(sk.shape)],
        out_specs=[pl.BlockSpec((tm // ql, D_MODEL, ql), lambda i: (i, 0, 0)),
                   big, big, big, big],
        out_shape=[jax.ShapeDtypeStruct((t // ql, D_MODEL, ql), BF16), shape(F32), shape(BF16),
                   shape(F32), shape(BF16)],
        scratch_shapes=[pltpu.VMEM((wqt.shape[0], tm), BF16),
                        pltpu.VMEM((2, PEER_TOPK, tm // LANES, PEER_HEADS, LANES), F32),
                        pltpu.VMEM((2, PEER_HEADS, PEER_N_KEYS, tm), F32),
                        pltpu.VMEM((2, PEER_HEADS, PEER_N_KEYS, tm), F32)],
        compiler_params=pltpu.CompilerParams(dimension_semantics=("arbitrary",),
                                             vmem_limit_bytes=VMEM_LIMIT),
        name="route",
    )(h1, nf, wqt, sk)


def _dense_kernel(u2q_ref, down_ref, down_next_ref, upt_ref, upt_prev_ref,
                  n_ref, r2_ref, e1_ref, e2_ref, h_ref, fn_ref, out_ref,
                  acc_ref, ht_ref, wt_ref, *, tm, te, ql):
    j = pl.program_id(1)
    last_j = pl.num_programs(1) - 1
    nc = te // PEER_N_KEYS
    nq = tm // ql

    def hidden(w_ref, q):
        ht_ref[q] = jnp.dot(w_ref[...], u2q_ref[q], preferred_element_type=F32)

    def project(w_ref, q):
        acc_ref[:, q * ql:(q + 1) * ql] += jnp.dot(w_ref[...], wt_ref[q],
                                                   preferred_element_type=F32)

    def key_rows(ref, h, cc, ln):
        tile = jnp.broadcast_to(ref[h, cc:cc + 1, ln], (BF16_ROWS, ql)).astype(BF16)
        return jnp.concatenate([tile] * (PEER_N_KEYS // BF16_ROWS), axis=0)

    @pl.when(j == 0)
    def _():
        acc_ref[...] = jnp.zeros_like(acc_ref)
        wt_ref[nq - 1] = jnp.zeros((te, ql), BF16)
        hidden(down_ref, 0)

    for q in range(nq):
        if q + 1 < nq:
            hidden(down_ref, q + 1)
        else:
            hidden(down_next_ref, 0)
        ln = slice(q * ql, (q + 1) * ql)
        for cc in range(nc):
            rows = slice(cc * PEER_N_KEYS, (cc + 1) * PEER_N_KEYS)
            g = jnp.zeros((PEER_N_KEYS, ql), BF16)
            for h in range(PEER_HEADS):
                gate = e2_ref[h, :, ln] * key_rows(e1_ref, h, cc, ln)
                g = g + jnp.where(r2_ref[h, :, ln] < key_rows(n_ref, h, cc, ln), gate,
                                  jnp.zeros((), BF16))
            hv = ht_ref[q, rows, :]
            act = hv * (1.0 + lax.erf(hv * np.float32(2.0 ** -0.5)))
            wt_ref[q, rows, :] = g * act.astype(BF16)
        if q == 0:
            project(upt_prev_ref, nq - 1)
        if q + 1 < nq:
            project(upt_ref, q)

    @pl.when(j == last_j)
    def _():
        project(upt_ref, nq - 1)
        out_ref[...] = _rms(h_ref[...] + acc_ref[...].T, fn_ref[...])


def _dense_call(u2q, down, upt, n, r2, e1, e2, h1, fn):
    t = h1.shape[0]
    n_exp = down.shape[0]
    tm, te, ql = TM_DENSE, TE_DENSE, QL_DENSE
    nc = te // PEER_N_KEYS
    nj = n_exp // te
    big = pl.BlockSpec((PEER_HEADS, PEER_N_KEYS, tm), lambda i, j: (0, 0, i))
    rows = pl.BlockSpec((PEER_HEADS, nc, tm), lambda i, j: (0, j, i))
    return pl.pallas_call(
        functools.partial(_dense_kernel, tm=tm, te=te, ql=ql),
        grid=(t // tm, nj),
        in_specs=[pl.BlockSpec((tm // ql, D_MODEL, ql), lambda i, j: (i, 0, 0)),
                  pl.BlockSpec((te, D_MODEL), lambda i, j: (j, 0)),
                  pl.BlockSpec((te, D_MODEL), lambda i, j: (jnp.minimum(j + 1, nj - 1), 0)),
                  pl.BlockSpec((D_MODEL, te), lambda i, j: (0, j)),
                  pl.BlockSpec((D_MODEL, te), lambda i, j: (0, jnp.maximum(j - 1, 0))),
                  rows, big, rows, big,
                  pl.BlockSpec((tm, D_MODEL), lambda i, j: (i, 0)),
                  pl.BlockSpec((1, D_MODEL), lambda i, j: (0, 0))],
        out_specs=pl.BlockSpec((tm, D_MODEL), lambda i, j: (i, 0)),
        out_shape=jax.ShapeDtypeStruct((t, D_MODEL), F32),
        scratch_shapes=[pltpu.VMEM((D_MODEL, tm), F32), pltpu.VMEM((tm // ql, te, ql), F32),
                        pltpu.VMEM((tm // ql, te, ql), BF16)],
        compiler_params=pltpu.CompilerParams(dimension_semantics=("arbitrary", "arbitrary"),
                                             vmem_limit_bytes=VMEM_LIMIT),
        name="dense",
    )(u2q, down, down, upt, upt, n, r2, e1, e2, h1, fn)


def _rope_lane_freqs():
    inv_freq = ROPE_THETA ** (-jnp.arange(ROT_HALF, dtype=F32) * (2.0 / ROT_DIM))
    lane = np.arange(LANES) % HEAD_DIM
    return jnp.where(lane < ROT_DIM, inv_freq[lane % ROT_HALF], 0.0).reshape(1, LANES).astype(F32)


def kernel(x, positions, norm_mix, w_in, conv_w, w_conv_out, w_attn_out, gate_bias, w_out,
           norm_ffn, peer_w_query, peer_sub_keys, peer_down, peer_up, final_norm):
    batch, seq, d = x.shape
    assert d == D_MODEL and w_in.shape[-1] == D_IN
    depth = w_in.shape[0]
    t = batch * seq
    h = x.reshape(t, d)
    pos2 = positions.reshape(t, 1)
    invf = _rope_lane_freqs()
    n_bcx = 3 * D_MODEL
    for layer in range(depth):
        w = w_in[layer]
        w_p = jnp.concatenate([w[:, :n_bcx], w[:, n_bcx + 3 * D_ATTN:],
                               w[:, n_bcx:n_bcx + 3 * D_ATTN]], axis=1).astype(BF16)
        yc, ga, qs, ks, vs = _proj_call(h, pos2, norm_mix[layer].reshape(1, d), invf, w_p,
                                        conv_w[layer], gate_bias[layer],
                                        w_conv_out[layer].astype(BF16), seq=seq)
        outs, lses = zip(*[_attn_call(qs[g], ks[g], vs[g], g, batch=batch, seq=seq)
                           for g in range(N_GROUPS)])
        h1 = _merge_call(h, yc, ga, outs, lses, gate_bias[layer],
                         w_attn_out[layer].astype(BF16), w_out[layer].astype(BF16))
        u2q, n, r2, e1, e2 = _route_call(
            h1, norm_ffn[layer].reshape(1, d), peer_w_query[layer].T.astype(BF16),
            peer_sub_keys[layer].astype(BF16))
        assert layer == depth - 1, "intermediate layers need a dense call without the final norm"
        h = _dense_call(u2q, peer_down[layer].astype(BF16), peer_up[layer].T.astype(BF16),
                        n, r2, e1, e2, h1, final_norm.reshape(1, d))
    return h.reshape(batch, seq, d)
```

```python
import functools

import numpy as np
import jax
import jax.numpy as jnp
from jax import lax
from jax.experimental import pallas as pl
from jax.experimental.pallas import tpu as pltpu

F32 = jnp.float32
BF16 = jnp.bfloat16

D_MODEL = 1024
HEAD_DIM = 64
HEADS_PER_GROUP = 8
DILATED_GROUPS = ((128, 1), (512, 4), (2048, 16))
N_GROUPS = len(DILATED_GROUPS)
D_GROUP = HEADS_PER_GROUP * HEAD_DIM
D_ATTN = N_GROUPS * D_GROUP
BLOCK = 128
ROT_DIM = HEAD_DIM // 4
ROT_HALF = ROT_DIM // 2
ROPE_THETA = 500000.0
PEER_HEADS = 8
PEER_N_KEYS = 128
PEER_TOPK = 16
PEER_D_HALF = 128
RMS_EPS = 1e-6
LANES = 128
SUBLANES = 8
BF16_ROWS = 2 * SUBLANES
RANK_CODE_BASE = 2.0 ** 120
VMEM_LIMIT = 60 * 1024 * 1024

COL_B, COL_C, COL_X, COL_GC, COL_GA, COL_QKV = 0, 1024, 2048, 3072, 4096, 5120
D_IN = COL_QKV + 3 * D_ATTN

LSE_LANES = LANES // HEADS_PER_GROUP
ATTN_BLOCKS_PER_STEP = 4
TM_PROJ = 512
TM_MERGE = 512
TM_ROUTE = 512
TM_DENSE = 1024
TE_DENSE = SUBLANES * PEER_N_KEYS
QL_DENSE = 256


def _const_spec(shape):
    nd = len(shape)
    return pl.BlockSpec(shape, lambda *_: (0,) * nd, pipeline_mode=pl.Buffered(1))


def _rms(x, g):
    return x * lax.rsqrt(jnp.mean(x * x, axis=-1, keepdims=True) + RMS_EPS) * g


def _proj_kernel(x_ref, pos_ref, nm_ref, invf_ref, w_ref, cw_ref, gb_ref, wco_ref, *rest,
                 tm, seq):
    yc_ref, ga_ref = rest[0], rest[1]
    qkv_refs = rest[2:2 + 3 * N_GROUPS]
    zbuf_ref, t_ref = rest[2 + 3 * N_GROUPS:]
    i = pl.program_id(0)
    u = _rms(x_ref[...], nm_ref[...]).astype(BF16)

    def mm(c0, n):
        return jnp.dot(u, w_ref[:, c0:c0 + n], preferred_element_type=F32)

    @pl.when((i * tm) % seq == 0)
    def _():
        zbuf_ref[0:8, :] = jnp.zeros((8, D_MODEL), F32)

    z = mm(COL_C, D_MODEL) * mm(COL_X, D_MODEL)
    zbuf_ref[8:tm + 8, :] = z
    cw = cw_ref[...]
    conv = cw[0:1] * zbuf_ref[6:tm + 6, :] + cw[1:2] * zbuf_ref[7:tm + 7, :] + cw[2:3] * z
    zbuf_ref[0:8, :] = zbuf_ref[tm:tm + 8, :]
    yb = (mm(COL_B, D_MODEL) * conv).astype(BF16)
    yc = jnp.dot(yb, wco_ref[...], preferred_element_type=F32)
    yc_ref[...] = (jax.nn.sigmoid(mm(COL_GC, D_MODEL) + gb_ref[0:1, :]) * yc).astype(BF16)
    ga_ref[...] = mm(COL_GA, D_MODEL).astype(BF16)

    ang = pos_ref[...].astype(F32) * invf_ref[...]
    cosv = jnp.cos(ang)
    sinv = jnp.sin(ang)
    lane = lax.broadcasted_iota(jnp.int32, (1, LANES), 1) % HEAD_DIM
    sin_lo = jnp.where(lane < ROT_HALF, -sinv, 0.0)
    sin_hi = jnp.where((lane >= ROT_HALF) & (lane < ROT_DIM), sinv, 0.0)
    for blk in range(3 * N_GROUPS):
        t = mm(COL_QKV + blk * D_GROUP, D_GROUP)
        dil = DILATED_GROUPS[blk % N_GROUPS][1]
        out_ref = qkv_refs[blk]
        for s in range(D_GROUP // LANES):
            ts = t[:, s * LANES:(s + 1) * LANES]
            if blk < 2 * N_GROUPS:
                ts = (ts * cosv + pltpu.roll(ts, LANES - ROT_HALF, 1) * sin_lo
                      + pltpu.roll(ts, ROT_HALF, 1) * sin_hi)
            if dil == 1:
                out_ref[0, :, s * LANES:(s + 1) * LANES] = ts.astype(BF16)
            else:
                t_ref[s] = ts
        if dil > 1:
            for r in range(dil):
                for s in range(D_GROUP // LANES):
                    out_ref[r, :, s * LANES:(s + 1) * LANES] = t_ref[
                        s, pl.ds(r, tm // dil, stride=dil), :].astype(BF16)


def _proj_call(x2, pos2, nm, invf, w_p, cw, gb, wco, *, seq):
    t = x2.shape[0]
    tm = TM_PROJ
    row = lambda n: pl.BlockSpec((tm, n), lambda i: (i, 0))
    dils = [d for _, d in DILATED_GROUPS] * 3
    qkv_specs = [pl.BlockSpec((d, tm // d, D_GROUP), lambda i: (0, i, 0)) for d in dils]
    qkv_shapes = [jax.ShapeDtypeStruct((d, t // d, D_GROUP), BF16) for d in dils]
    res = pl.pallas_call(
        functools.partial(_proj_kernel, tm=tm, seq=seq),
        grid=(t // tm,),
        in_specs=[row(D_MODEL), row(1), _const_spec((1, D_MODEL)), _const_spec((1, LANES)),
                  _const_spec((D_MODEL, D_IN)), _const_spec((3, D_MODEL)),
                  _const_spec((2, D_MODEL)), _const_spec((D_MODEL, D_MODEL))],
        out_specs=[row(D_MODEL), row(D_MODEL)] + qkv_specs,
        out_shape=[jax.ShapeDtypeStruct((t, D_MODEL), BF16),
                   jax.ShapeDtypeStruct((t, D_MODEL), BF16)] + qkv_shapes,
        scratch_shapes=[pltpu.VMEM((tm + 8, D_MODEL), F32),
                        pltpu.VMEM((D_GROUP // LANES, tm, LANES), F32)],
        compiler_params=pltpu.CompilerParams(dimension_semantics=("arbitrary",),
                                             vmem_limit_bytes=VMEM_LIMIT),
        name="proj",
    )(x2, pos2, nm, invf, w_p, cw, gb, wco)
    yc, ga = res[0], res[1]
    qs, ks, vs = res[2:2 + N_GROUPS], res[2 + N_GROUPS:2 + 2 * N_GROUPS], res[2 + 2 * N_GROUPS:]
    return yc, ga, qs, ks, vs


def _attn_kernel(q_ref, kp_ref, kc_ref, vp_ref, vc_ref, o_ref, lse_ref, *, steps, nsub):
    n = pl.program_id(2)
    qi = lax.broadcasted_iota(jnp.int32, (BLOCK, 2 * BLOCK), 0)
    ki = lax.broadcasted_iota(jnp.int32, (BLOCK, 2 * BLOCK), 1)
    dist = BLOCK + qi - ki
    band = (dist >= 0) & (dist <= steps)
    valid_first = band & ((n > 0) | (ki >= BLOCK))
    first = lax.broadcasted_iota(jnp.int32, (1, LANES), 1) < HEAD_DIM
    ones = jnp.ones((2 * BLOCK, LANES), BF16)
    scale = HEAD_DIM ** -0.5
    n_slab = D_GROUP // LANES
    slabs = [slice(hp * LANES, (hp + 1) * LANES) for hp in range(n_slab)]
    picks = (first, jnp.logical_not(first))

    def keys_of(prev_ref, cur_ref, sub, sl):
        if sub == 0:
            return jnp.concatenate([prev_ref[:, sl], cur_ref[0:BLOCK, sl]], axis=0)
        return cur_ref[(sub - 1) * BLOCK:(sub + 1) * BLOCK, sl]

    units = [(sub, hp) for sub in range(nsub) for hp in range(n_slab)]
    scores = []
    for sub, hp in units:
        sl = slabs[hp]
        q2 = q_ref[sub * BLOCK:(sub + 1) * BLOCK, sl] * scale
        k2 = keys_of(kp_ref, kc_ref, sub, sl)
        for pick in picks:
            qh = jnp.where(pick, q2, jnp.zeros((), BF16))
            s = lax.dot_general(qh, k2, (((1,), (1,)), ((), ())), preferred_element_type=F32)
            scores.append(jnp.where(valid_first if sub == 0 else band, s, -jnp.inf))
    maxes = [jnp.max(s, axis=-1, keepdims=True) for s in scores]
    probs = [jnp.exp(s - m).astype(BF16) for s, m in zip(scores, maxes)]
    head_of_lane = lax.broadcasted_iota(jnp.int32, (1, LANES), 1) // LSE_LANES
    lse = [None] * nsub
    for u, (sub, hp) in enumerate(units):
        sl = slabs[hp]
        rows = slice(sub * BLOCK, (sub + 1) * BLOCK)
        v2 = keys_of(vp_ref, vc_ref, sub, sl)
        pa, pb = probs[2 * u], probs[2 * u + 1]
        la = jnp.dot(pa, ones, preferred_element_type=F32)
        lb = jnp.dot(pb, ones, preferred_element_type=F32)
        o = jnp.where(first, jnp.dot(pa, v2, preferred_element_type=F32),
                      jnp.dot(pb, v2, preferred_element_type=F32))
        o_ref[rows, sl] = (o / jnp.where(first, la, lb)).astype(BF16)
        for k, (m, l) in enumerate(((maxes[2 * u], la), (maxes[2 * u + 1], lb))):
            val = m + jnp.log(l)
            lse[sub] = val if lse[sub] is None else jnp.where(
                head_of_lane == 2 * hp + k, val, lse[sub])
    for sub in range(nsub):
        lse_ref[sub * BLOCK:(sub + 1) * BLOCK, :] = lse[sub]


def _attn_call(q, k, v, g, *, batch, seq):
    window, dil = DILATED_GROUPS[g]
    steps = window // dil
    nsub = ATTN_BLOCKS_PER_STEP
    assert steps <= BLOCK and seq % (dil * BLOCK * nsub) == 0
    nb = seq // (dil * BLOCK)
    cur = pl.BlockSpec((None, nsub * BLOCK, D_GROUP), lambda b, r, n: (r, b * (nb // nsub) + n, 0))
    lse_spec = pl.BlockSpec((None, nsub * BLOCK, LANES), lambda b, r, n: (r, b * (nb // nsub) + n, 0))
    prev = pl.BlockSpec((None, BLOCK, D_GROUP),
                        lambda b, r, n: (r, b * nb + jnp.maximum(nsub * n - 1, 0), 0))
    return pl.pallas_call(
        functools.partial(_attn_kernel, steps=steps, nsub=nsub),
        grid=(batch, dil, nb // nsub),
        in_specs=[cur, prev, cur, prev, cur],
        out_specs=[cur, lse_spec],
        out_shape=[jax.ShapeDtypeStruct(q.shape, BF16),
                   jax.ShapeDtypeStruct(q.shape[:2] + (LANES,), F32)],
        compiler_params=pltpu.CompilerParams(
            dimension_semantics=("arbitrary", "arbitrary", "arbitrary")),
        name=f"attn{g}",
    )(q, k, k, v, v)


def _merge_kernel(x_ref, yc_ref, ga_ref, o0_ref, o1_ref, o2_ref, l0_ref, l1_ref, l2_ref,
                  gb_ref, wao_ref, wo_ref, h_ref, *scratch, tm):
    def natural(ref, g, scr, width):
        dil = DILATED_GROUPS[g][1]
        if dil == 1:
            return ref[0].astype(F32)
        n_slab = width // LANES
        for r in range(dil):
            for s in range(n_slab):
                scr[s, pl.ds(r, tm // dil, stride=dil), :] = ref[
                    r, :, s * LANES:(s + 1) * LANES].astype(F32)
        return jnp.concatenate([scr[s] for s in range(n_slab)], axis=1)

    scr = iter(scratch)
    o_refs, l_refs = (o0_ref, o1_ref, o2_ref), (l0_ref, l1_ref, l2_ref)
    dilated = [DILATED_GROUPS[g][1] > 1 for g in range(N_GROUPS)]
    ls = [natural(l_refs[g], g, next(scr) if dilated[g] else None, LANES)
          for g in range(N_GROUPS)]
    os_ = [natural(o_refs[g], g, next(scr) if dilated[g] else None, D_GROUP)
           for g in range(N_GROUPS)]
    mx = functools.reduce(jnp.maximum, ls)
    es = [jnp.exp(l - mx) for l in ls]
    inv = 1.0 / sum(es)
    row = lax.broadcasted_iota(jnp.int32, (LANES, D_GROUP), 0)
    col = lax.broadcasted_iota(jnp.int32, (LANES, D_GROUP), 1)
    spread = jnp.where(row == (col // HEAD_DIM) * LSE_LANES, 1.0, 0.0).astype(BF16)
    o = sum(jnp.dot((e * inv).astype(BF16), spread, preferred_element_type=F32) * ov
            for e, ov in zip(es, os_))
    ya = jnp.dot(o.astype(BF16), wao_ref[...], preferred_element_type=F32)
    merged = (yc_ref[...].astype(F32)
              + jax.nn.sigmoid(ga_ref[...].astype(F32) + gb_ref[1:2, :]) * ya)
    h_ref[...] = x_ref[...] + jnp.dot(merged.astype(BF16), wo_ref[...],
                                      preferred_element_type=F32)


def _merge_call(x2, yc, ga, outs, lses, gb, wao, wo):
    t = x2.shape[0]
    tm = TM_MERGE
    row = lambda n: pl.BlockSpec((tm, n), lambda i: (i, 0))
    dils = [d for _, d in DILATED_GROUPS]
    grp = lambda w: [pl.BlockSpec((d, tm // d, w), lambda i: (0, i, 0)) for d in dils]
    scratch = ([pltpu.VMEM((1, tm, LANES), F32) for d in dils if d > 1]
               + [pltpu.VMEM((D_GROUP // LANES, tm, LANES), F32) for d in dils if d > 1])
    return pl.pallas_call(
        functools.partial(_merge_kernel, tm=tm),
        grid=(t // tm,),
        in_specs=[row(D_MODEL), row(D_MODEL), row(D_MODEL)] + grp(D_GROUP) + grp(LANES)
                 + [_const_spec((2, D_MODEL)), _const_spec((D_GROUP, D_MODEL)),
                    _const_spec((D_MODEL, D_MODEL))],
        out_specs=row(D_MODEL),
        out_shape=jax.ShapeDtypeStruct((t, D_MODEL), F32),
        scratch_shapes=scratch,
        compiler_params=pltpu.CompilerParams(dimension_semantics=("arbitrary",),
                                             vmem_limit_bytes=VMEM_LIMIT),
        name="merge",
    )(x2, yc, ga, *outs, *lses, gb, wao, wo)


def _pair_candidates():
    return [(a, b) for a in range(PEER_TOPK) for b in range(PEER_TOPK)
            if (a + 1) * (b + 1) <= PEER_TOPK]


def _rank_code(r):
    return -RANK_CODE_BASE * (1.0 + r / 32.0)


def _route_kernel(h_ref, nf_ref, wqt_ref, sk_ref,
                  u2q_ref, n_ref, r2_ref, e1_ref, e2_ref,
                  qt_ref, ts_ref, s_ref, rk_ref, *, tm, ql):
    u2 = _rms(h_ref[...], nf_ref[...])
    u2t = u2.T.astype(BF16)
    for qq in range(tm // ql):
        u2q_ref[qq] = u2t[:, qq * ql:(qq + 1) * ql]
    qt_ref[...] = jnp.dot(wqt_ref[...], u2t, preferred_element_type=F32).astype(BF16)
    n_lb = tm // LANES

    def head_body(h, carry):
        for c in range(2):
            r0 = pl.multiple_of(h * (2 * PEER_D_HALF) + c * PEER_D_HALF, PEER_D_HALF)
            s_ref[c, h] = jnp.dot(sk_ref[h, c], qt_ref[pl.ds(r0, PEER_D_HALF), :],
                                  preferred_element_type=F32)
            for lb in range(n_lb):
                ln = slice(lb * LANES, (lb + 1) * LANES)
                cur = s_ref[c, h, :, ln]
                for r in range(PEER_TOPK):
                    m = jnp.max(cur, axis=0, keepdims=True)
                    ts_ref[c, r, lb, pl.ds(h, 1), :] = m
                    cur = jnp.where(cur >= m, _rank_code(r), cur)
                rk_ref[c, h, :, ln] = cur
        return carry

    lax.fori_loop(0, PEER_HEADS, head_body, 0)

    def top_rows(c, r):
        return jnp.concatenate([ts_ref[c, r, lb] for lb in range(n_lb)], axis=1)

    top1 = [top_rows(0, r) for r in range(PEER_TOPK)]
    top2 = [top_rows(1, r) for r in range(PEER_TOPK)]
    pairs = _pair_candidates()
    cands = [top1[a] + top2[b] for a, b in pairs]
    cmax = top1[0] + top2[0]
    cur = list(cands)
    for r in range(PEER_TOPK):
        m = functools.reduce(jnp.maximum, cur)
        if r + 1 < PEER_TOPK:
            cur = [jnp.where(c >= m, -jnp.inf, c) for c in cur]
    tau = m
    sel = [c >= tau for c in cands]
    z = functools.reduce(jnp.add, [jnp.where(s, jnp.exp(c - cmax), 0.0)
                                   for s, c in zip(sel, cands)])
    half_inv_z = 0.5 / z
    cnt = []
    for a in range(PEER_TOPK):
        cnt.append(functools.reduce(
            jnp.add, [jnp.where(s, 1.0, 0.0) for s, (pa, _) in zip(sel, pairs) if pa == a]))

    for h in range(PEER_HEADS):
        row = slice(h, h + 1)
        for lb in range(n_lb):
            ln = slice(lb * LANES, (lb + 1) * LANES)
            code1 = rk_ref[0, h, :, ln]
            n = jnp.zeros(code1.shape, F32)
            for a in range(PEER_TOPK):
                n = jnp.where(code1 == _rank_code(a), cnt[a][row, ln], n)
            n_ref[h, :, ln] = n
            e1_ref[h, :, ln] = jnp.exp(s_ref[0, h, :, ln] - top1[0][row, ln]) * half_inv_z[row, ln]
            code2 = rk_ref[1, h, :, ln]
            rank2 = jnp.where(code2 <= -RANK_CODE_BASE,
                              (code2 * (-1.0 / RANK_CODE_BASE) - 1.0) * 32.0, float(PEER_TOPK))
            r2_ref[h, :, ln] = rank2.astype(BF16)
            e2_ref[h, :, ln] = jnp.exp(s_ref[1, h, :, ln] - top2[0][row, ln]).astype(BF16)


def _route_call(h1, nf, wqt, sk):
    t = h1.shape[0]
    tm, ql = TM_ROUTE, QL_DENSE
    big = pl.BlockSpec((PEER_HEADS, PEER_N_KEYS, tm), lambda i: (0, 0, i))
    shape = lambda dt: jax.ShapeDtypeStruct((PEER_HEADS, PEER_N_KEYS, t), dt)
    return pl.pallas_call(
        functools.partial(_route_kernel, tm=tm, ql=ql),
        grid=(t // tm,),
        in_specs=[pl.BlockSpec((tm, D_MODEL), lambda i: (i, 0)), _const_spec((1, D_MODEL)),
                  _const_spec(wqt.shape), _const_spec(sk.shape)],
        out_specs=[pl.BlockSpec((tm // ql, D_MODEL, ql), lambda i: (i, 0, 0)),
                   big, big, big, big],
        out_shape=[jax.ShapeDtypeStruct((t // ql, D_MODEL, ql), BF16), shape(F32), shape(BF16),
                   shape(F32), shape(BF16)],
        scratch_shapes=[pltpu.VMEM((wqt.shape[0], tm), BF16),
                        pltpu.VMEM((2, PEER_TOPK, tm // LANES, PEER_HEADS, LANES), F32),
                        pltpu.VMEM((2, PEER_HEADS, PEER_N_KEYS, tm), F32),
                        pltpu.VMEM((2, PEER_HEADS, PEER_N_KEYS, tm), F32)],
        compiler_params=pltpu.CompilerParams(dimension_semantics=("arbitrary",),
                                             vmem_limit_bytes=VMEM_LIMIT),
        name="route",
    )(h1, nf, wqt, sk)


def _dense_kernel(u2q_ref, down_ref, down_next_ref, upt_ref, upt_prev_ref,
                  n_ref, r2_ref, e1_ref, e2_ref, h_ref, fn_ref, out_ref,
                  acc_ref, ht_ref, wt_ref, *, tm, te, ql):
    j = pl.program_id(1)
    last_j = pl.num_programs(1) - 1
    nc = te // PEER_N_KEYS
    nq = tm // ql

    def hidden(w_ref, q):
        ht_ref[q] = jnp.dot(w_ref[...], u2q_ref[q], preferred_element_type=F32)

    def project(w_ref, q):
        acc_ref[:, q * ql:(q + 1) * ql] += jnp.dot(w_ref[...], wt_ref[q],
                                                   preferred_element_type=F32)

    def key_rows(ref, h, cc, ln):
        tile = jnp.broadcast_to(ref[h, cc:cc + 1, ln], (BF16_ROWS, ql)).astype(BF16)
        return jnp.concatenate([tile] * (PEER_N_KEYS // BF16_ROWS), axis=0)

    @pl.when(j == 0)
    def _():
        acc_ref[...] = jnp.zeros_like(acc_ref)
        wt_ref[nq - 1] = jnp.zeros((te, ql), BF16)
        hidden(down_ref, 0)

    for q in range(nq):
        if q + 1 < nq:
            hidden(down_ref, q + 1)
        else:
            hidden(down_next_ref, 0)
        ln = slice(q * ql, (q + 1) * ql)
        for cc in range(nc):
            rows = slice(cc * PEER_N_KEYS, (cc + 1) * PEER_N_KEYS)
            g = jnp.zeros((PEER_N_KEYS, ql), BF16)
            for h in range(PEER_HEADS):
                gate = e2_ref[h, :, ln] * key_rows(e1_ref, h, cc, ln)
                g = g + jnp.where(r2_ref[h, :, ln] < key_rows(n_ref, h, cc, ln), gate,
                                  jnp.zeros((), BF16))
            hv = ht_ref[q, rows, :]
            act = hv * (1.0 + lax.erf(hv * np.float32(2.0 ** -0.5)))
            wt_ref[q, rows, :] = g * act.astype(BF16)
        if q == 0:
            project(upt_prev_ref, nq - 1)
        if q + 1 < nq:
            project(upt_ref, q)

    @pl.when(j == last_j)
    def _():
        project(upt_ref, nq - 1)
        out_ref[...] = _rms(h_ref[...] + acc_ref[...].T, fn_ref[...])


def _dense_call(u2q, down, upt, n, r2, e1, e2, h1, fn):
    t = h1.shape[0]
    n_exp = down.shape[0]
    tm, te, ql = TM_DENSE, TE_DENSE, QL_DENSE
    nc = te // PEER_N_KEYS
    nj = n_exp // te
    big = pl.BlockSpec((PEER_HEADS, PEER_N_KEYS, tm), lambda i, j: (0, 0, i))
    rows = pl.BlockSpec((PEER_HEADS, nc, tm), lambda i, j: (0, j, i))
    return pl.pallas_call(
        functools.partial(_dense_kernel, tm=tm, te=te, ql=ql),
        grid=(t // tm, nj),
        in_specs=[pl.BlockSpec((tm // ql, D_MODEL, ql), lambda i, j: (i, 0, 0)),
                  pl.BlockSpec((te, D_MODEL), lambda i, j: (j, 0)),
                  pl.BlockSpec((te, D_MODEL), lambda i, j: (jnp.minimum(j + 1, nj - 1), 0)),
                  pl.BlockSpec((D_MODEL, te), lambda i, j: (0, j)),
                  pl.BlockSpec((D_MODEL, te), lambda i, j: (0, jnp.maximum(j - 1, 0))),
                  rows, big, rows, big,
                  pl.BlockSpec((tm, D_MODEL), lambda i, j: (i, 0)),
                  pl.BlockSpec((1, D_MODEL), lambda i, j: (0, 0))],
        out_specs=pl.BlockSpec((tm, D_MODEL), lambda i, j: (i, 0)),
        out_shape=jax.ShapeDtypeStruct((t, D_MODEL), F32),
        scratch_shapes=[pltpu.VMEM((D_MODEL, tm), F32), pltpu.VMEM((tm // ql, te, ql), F32),
                        pltpu.VMEM((tm // ql, te, ql), BF16)],
        compiler_params=pltpu.CompilerParams(dimension_semantics=("arbitrary", "arbitrary"),
                                             vmem_limit_bytes=VMEM_LIMIT),
        name="dense",
    )(u2q, down, down, upt, upt, n, r2, e1, e2, h1, fn)


def _rope_lane_freqs():
    inv_freq = ROPE_THETA ** (-jnp.arange(ROT_HALF, dtype=F32) * (2.0 / ROT_DIM))
    lane = np.arange(LANES) % HEAD_DIM
    return jnp.where(lane < ROT_DIM, inv_freq[lane % ROT_HALF], 0.0).reshape(1, LANES).astype(F32)


def kernel(x, positions, norm_mix, w_in, conv_w, w_conv_out, w_attn_out, gate_bias, w_out,
           norm_ffn, peer_w_query, peer_sub_keys, peer_down, peer_up, final_norm):
    batch, seq, d = x.shape
    assert d == D_MODEL and w_in.shape[-1] == D_IN
    depth = w_in.shape[0]
    t = batch * seq
    h = x.reshape(t, d)
    pos2 = positions.reshape(t, 1)
    invf = _rope_lane_freqs()
    n_bcx = 3 * D_MODEL
    for layer in range(depth):
        w = w_in[layer]
        w_p = jnp.concatenate([w[:, :n_bcx], w[:, n_bcx + 3 * D_ATTN:],
                               w[:, n_bcx:n_bcx + 3 * D_ATTN]], axis=1).astype(BF16)
        yc, ga, qs, ks, vs = _proj_call(h, pos2, norm_mix[layer].reshape(1, d), invf, w_p,
                                        conv_w[layer], gate_bias[layer],
                                        w_conv_out[layer].astype(BF16), seq=seq)
        outs, lses = zip(*[_attn_call(qs[g], ks[g], vs[g], g, batch=batch, seq=seq)
                           for g in range(N_GROUPS)])
        h1 = _merge_call(h, yc, ga, outs, lses, gate_bias[layer],
                         w_attn_out[layer].astype(BF16), w_out[layer].astype(BF16))
        u2q, n, r2, e1, e2 = _route_call(
            h1, norm_ffn[layer].reshape(1, d), peer_w_query[layer].T.astype(BF16),
            peer_sub_keys[layer].astype(BF16))
        assert layer == depth - 1, "intermediate layers need a dense call without the final norm"
        h = _dense_call(u2q, peer_down[layer].astype(BF16), peer_up[layer].T.astype(BF16),
                        n, r2, e1, e2, h1, final_norm.reshape(1, d))
    return h.reshape(batch, seq, d)
```

```python
import functools

import numpy as np
import jax
import jax.numpy as jnp
from jax import lax
from jax.experimental import pallas as pl
from jax.experimental.pallas import tpu as pltpu

F32 = jnp.float32
BF16 = jnp.bfloat16

D_MODEL = 1024
HEAD_DIM = 64
HEADS_PER_GROUP = 8
DILATED_GROUPS = ((128, 1), (512, 4), (2048, 16))
N_GROUPS = len(DILATED_GROUPS)
D_GROUP = HEADS_PER_GROUP * HEAD_DIM
D_ATTN = N_GROUPS * D_GROUP
BLOCK = 128
ROT_DIM = HEAD_DIM // 4
ROT_HALF = ROT_DIM // 2
ROPE_THETA = 500000.0
PEER_HEADS = 8
PEER_N_KEYS = 128
PEER_TOPK = 16
PEER_D_HALF = 128
RMS_EPS = 1e-6
LANES = 128
SUBLANES = 8
BF16_ROWS = 2 * SUBLANES
RANK_CODE_BASE = 2.0 ** 120
VMEM_LIMIT = 60 * 1024 * 1024

COL_B, COL_C, COL_X, COL_QKV = 0, D_MODEL, 2 * D_MODEL, 3 * D_MODEL
COL_GC = COL_QKV + 3 * D_ATTN
COL_GA = COL_GC + D_MODEL
D_IN = COL_GA + D_MODEL

LSE_LANES = LANES // HEADS_PER_GROUP
ATTN_BLOCKS_PER_STEP = 4
TM_PROJ = 512
TM_MERGE = 512
TM_ROUTE = 512
TM_DENSE = 1024
TE_DENSE = SUBLANES * PEER_N_KEYS
QL_DENSE = 256


def _const_spec(shape):
    nd = len(shape)
    return pl.BlockSpec(shape, lambda *_: (0,) * nd, pipeline_mode=pl.Buffered(1))


def _rms(x, g):
    return x * lax.rsqrt(jnp.mean(x * x, axis=-1, keepdims=True) + RMS_EPS) * g


def _proj_kernel(x_ref, pos_ref, nm_ref, invf_ref, w_ref, cw_ref, gb_ref, wco_ref, *rest,
                 tm, seq):
    yc_ref, ga_ref = rest[0], rest[1]
    qkv_refs = rest[2:2 + 3 * N_GROUPS]
    zbuf_ref, t_ref = rest[2 + 3 * N_GROUPS:]
    i = pl.program_id(0)
    u = _rms(x_ref[...], nm_ref[...]).astype(BF16)

    def mm(c0, n):
        return jnp.dot(u, w_ref[:, c0:c0 + n], preferred_element_type=F32)

    @pl.when((i * tm) % seq == 0)
    def _():
        zbuf_ref[0:8, :] = jnp.zeros((8, D_MODEL), F32)

    z = mm(COL_C, D_MODEL) * mm(COL_X, D_MODEL)
    zbuf_ref[8:tm + 8, :] = z
    cw = cw_ref[...]
    conv = cw[0:1] * zbuf_ref[6:tm + 6, :] + cw[1:2] * zbuf_ref[7:tm + 7, :] + cw[2:3] * z
    zbuf_ref[0:8, :] = zbuf_ref[tm:tm + 8, :]
    yb = (mm(COL_B, D_MODEL) * conv).astype(BF16)
    yc = jnp.dot(yb, wco_ref[...], preferred_element_type=F32)
    yc_ref[...] = (jax.nn.sigmoid(mm(COL_GC, D_MODEL) + gb_ref[0:1, :]) * yc).astype(BF16)
    ga_ref[...] = mm(COL_GA, D_MODEL).astype(BF16)

    ang = pos_ref[...].astype(F32) * invf_ref[...]
    cosv = jnp.cos(ang)
    sinv = jnp.sin(ang)
    lane = lax.broadcasted_iota(jnp.int32, (1, LANES), 1) % HEAD_DIM
    sin_lo = jnp.where(lane < ROT_HALF, -sinv, 0.0)
    sin_hi = jnp.where((lane >= ROT_HALF) & (lane < ROT_DIM), sinv, 0.0)
    for blk in range(3 * N_GROUPS):
        t = mm(COL_QKV + blk * D_GROUP, D_GROUP)
        dil = DILATED_GROUPS[blk % N_GROUPS][1]
        out_ref = qkv_refs[blk]
        for s in range(D_GROUP // LANES):
            ts = t[:, s * LANES:(s + 1) * LANES]
            if blk < 2 * N_GROUPS:
                ts = (ts * cosv + pltpu.roll(ts, LANES - ROT_HALF, 1) * sin_lo
                      + pltpu.roll(ts, ROT_HALF, 1) * sin_hi)
            if dil == 1:
                out_ref[0, :, s * LANES:(s + 1) * LANES] = ts.astype(BF16)
            else:
                t_ref[s] = ts
        if dil > 1:
            for r in range(dil):
                for s in range(D_GROUP // LANES):
                    out_ref[r, :, s * LANES:(s + 1) * LANES] = t_ref[
                        s, pl.ds(r, tm // dil, stride=dil), :].astype(BF16)


def _proj_call(x2, pos2, nm, invf, w_p, cw, gb, wco, *, seq):
    t = x2.shape[0]
    tm = TM_PROJ
    row = lambda n: pl.BlockSpec((tm, n), lambda i: (i, 0))
    dils = [d for _, d in DILATED_GROUPS] * 3
    qkv_specs = [pl.BlockSpec((d, tm // d, D_GROUP), lambda i: (0, i, 0)) for d in dils]
    qkv_shapes = [jax.ShapeDtypeStruct((d, t // d, D_GROUP), BF16) for d in dils]
    res = pl.pallas_call(
        functools.partial(_proj_kernel, tm=tm, seq=seq),
        grid=(t // tm,),
        in_specs=[row(D_MODEL), row(1), _const_spec((1, D_MODEL)), _const_spec((1, LANES)),
                  _const_spec((D_MODEL, D_IN)), _const_spec((3, D_MODEL)),
                  _const_spec((2, D_MODEL)), _const_spec((D_MODEL, D_MODEL))],
        out_specs=[row(D_MODEL), row(D_MODEL)] + qkv_specs,
        out_shape=[jax.ShapeDtypeStruct((t, D_MODEL), BF16),
                   jax.ShapeDtypeStruct((t, D_MODEL), BF16)] + qkv_shapes,
        scratch_shapes=[pltpu.VMEM((tm + 8, D_MODEL), F32),
                        pltpu.VMEM((D_GROUP // LANES, tm, LANES), F32)],
        compiler_params=pltpu.CompilerParams(dimension_semantics=("arbitrary",),
                                             vmem_limit_bytes=VMEM_LIMIT),
        name="proj",
    )(x2, pos2, nm, invf, w_p, cw, gb, wco)
    yc, ga = res[0], res[1]
    qs, ks, vs = res[2:2 + N_GROUPS], res[2 + N_GROUPS:2 + 2 * N_GROUPS], res[2 + 2 * N_GROUPS:]
    return yc, ga, qs, ks, vs


def _attn_kernel(q_ref, kp_ref, kc_ref, vp_ref, vc_ref, o_ref, lse_ref, *, steps, nsub):
    n = pl.program_id(2)
    qi = lax.broadcasted_iota(jnp.int32, (BLOCK, 2 * BLOCK), 0)
    ki = lax.broadcasted_iota(jnp.int32, (BLOCK, 2 * BLOCK), 1)
    dist = BLOCK + qi - ki
    band = (dist >= 0) & (dist <= steps)
    valid_first = band & ((n > 0) | (ki >= BLOCK))
    first = lax.broadcasted_iota(jnp.int32, (1, LANES), 1) < HEAD_DIM
    ones = jnp.ones((2 * BLOCK, LANES), BF16)
    scale = HEAD_DIM ** -0.5
    n_slab = D_GROUP // LANES
    slabs = [slice(hp * LANES, (hp + 1) * LANES) for hp in range(n_slab)]
    picks = (first, jnp.logical_not(first))

    def keys_of(prev_ref, cur_ref, sub, sl):
        if sub == 0:
            return jnp.concatenate([prev_ref[:, sl], cur_ref[0:BLOCK, sl]], axis=0)
        return cur_ref[(sub - 1) * BLOCK:(sub + 1) * BLOCK, sl]

    units = [(sub, hp) for sub in range(nsub) for hp in range(n_slab)]
    scores = []
    for sub, hp in units:
        sl = slabs[hp]
        q2 = q_ref[sub * BLOCK:(sub + 1) * BLOCK, sl] * scale
        k2 = keys_of(kp_ref, kc_ref, sub, sl)
        for pick in picks:
            qh = jnp.where(pick, q2, jnp.zeros((), BF16))
            s = lax.dot_general(qh, k2, (((1,), (1,)), ((), ())), preferred_element_type=F32)
            scores.append(jnp.where(valid_first if sub == 0 else band, s, -jnp.inf))
    maxes = [jnp.max(s, axis=-1, keepdims=True) for s in scores]
    probs = [jnp.exp(s - m).astype(BF16) for s, m in zip(scores, maxes)]
    head_of_lane = lax.broadcasted_iota(jnp.int32, (1, LANES), 1) // LSE_LANES
    lse = [None] * nsub
    for u, (sub, hp) in enumerate(units):
        sl = slabs[hp]
        rows = slice(sub * BLOCK, (sub + 1) * BLOCK)
        v2 = keys_of(vp_ref, vc_ref, sub, sl)
        pa, pb = probs[2 * u], probs[2 * u + 1]
        la = jnp.dot(pa, ones, preferred_element_type=F32)
        lb = jnp.dot(pb, ones, preferred_element_type=F32)
        o = jnp.where(first, jnp.dot(pa, v2, preferred_element_type=F32),
                      jnp.dot(pb, v2, preferred_element_type=F32))
        o_ref[rows, sl] = (o / jnp.where(first, la, lb)).astype(BF16)
        for k, (m, l) in enumerate(((maxes[2 * u], la), (maxes[2 * u + 1], lb))):
            val = m + jnp.log(l)
            lse[sub] = val if lse[sub] is None else jnp.where(
                head_of_lane == 2 * hp + k, val, lse[sub])
    for sub in range(nsub):
        lse_ref[sub * BLOCK:(sub + 1) * BLOCK, :] = lse[sub]


def _attn_call(q, k, v, g, *, batch, seq):
    window, dil = DILATED_GROUPS[g]
    steps = window // dil
    nsub = ATTN_BLOCKS_PER_STEP
    assert steps <= BLOCK and seq % (dil * BLOCK * nsub) == 0
    nb = seq // (dil * BLOCK)
    cur = pl.BlockSpec((None, nsub * BLOCK, D_GROUP), lambda b, r, n: (r, b * (nb // nsub) + n, 0))
    lse_spec = pl.BlockSpec((None, nsub * BLOCK, LANES), lambda b, r, n: (r, b * (nb // nsub) + n, 0))
    prev = pl.BlockSpec((None, BLOCK, D_GROUP),
                        lambda b, r, n: (r, b * nb + jnp.maximum(nsub * n - 1, 0), 0))
    return pl.pallas_call(
        functools.partial(_attn_kernel, steps=steps, nsub=nsub),
        grid=(batch, dil, nb // nsub),
        in_specs=[cur, prev, cur, prev, cur],
        out_specs=[cur, lse_spec],
        out_shape=[jax.ShapeDtypeStruct(q.shape, BF16),
                   jax.ShapeDtypeStruct(q.shape[:2] + (LANES,), F32)],
        compiler_params=pltpu.CompilerParams(
            dimension_semantics=("arbitrary", "arbitrary", "arbitrary")),
        name=f"attn{g}",
    )(q, k, k, v, v)


def _merge_kernel(x_ref, yc_ref, ga_ref, o0_ref, o1_ref, o2_ref, l0_ref, l1_ref, l2_ref,
                  gb_ref, wao_ref, wo_ref, h_ref, *scratch, tm):
    def natural(ref, g, scr, width):
        dil = DILATED_GROUPS[g][1]
        if dil == 1:
            return ref[0].astype(F32)
        n_slab = width // LANES
        for r in range(dil):
            for s in range(n_slab):
                scr[s, pl.ds(r, tm // dil, stride=dil), :] = ref[
                    r, :, s * LANES:(s + 1) * LANES].astype(F32)
        return jnp.concatenate([scr[s] for s in range(n_slab)], axis=1)

    scr = iter(scratch)
    o_refs, l_refs = (o0_ref, o1_ref, o2_ref), (l0_ref, l1_ref, l2_ref)
    dilated = [DILATED_GROUPS[g][1] > 1 for g in range(N_GROUPS)]
    ls = [natural(l_refs[g], g, next(scr) if dilated[g] else None, LANES)
          for g in range(N_GROUPS)]
    os_ = [natural(o_refs[g], g, next(scr) if dilated[g] else None, D_GROUP)
           for g in range(N_GROUPS)]
    mx = functools.reduce(jnp.maximum, ls)
    es = [jnp.exp(l - mx) for l in ls]
    inv = 1.0 / sum(es)
    row = lax.broadcasted_iota(jnp.int32, (LANES, D_GROUP), 0)
    col = lax.broadcasted_iota(jnp.int32, (LANES, D_GROUP), 1)
    spread = jnp.where(row == (col // HEAD_DIM) * LSE_LANES, 1.0, 0.0).astype(BF16)
    o = sum(jnp.dot((e * inv).astype(BF16), spread, preferred_element_type=F32) * ov
            for e, ov in zip(es, os_))
    ya = jnp.dot(o.astype(BF16), wao_ref[...], preferred_element_type=F32)
    merged = (yc_ref[...].astype(F32)
              + jax.nn.sigmoid(ga_ref[...].astype(F32) + gb_ref[1:2, :]) * ya)
    h_ref[...] = x_ref[...] + jnp.dot(merged.astype(BF16), wo_ref[...],
                                      preferred_element_type=F32)


def _merge_call(x2, yc, ga, outs, lses, gb, wao, wo):
    t = x2.shape[0]
    tm = TM_MERGE
    row = lambda n: pl.BlockSpec((tm, n), lambda i: (i, 0))
    dils = [d for _, d in DILATED_GROUPS]
    grp = lambda w: [pl.BlockSpec((d, tm // d, w), lambda i: (0, i, 0)) for d in dils]
    scratch = ([pltpu.VMEM((1, tm, LANES), F32) for d in dils if d > 1]
               + [pltpu.VMEM((D_GROUP // LANES, tm, LANES), F32) for d in dils if d > 1])
    return pl.pallas_call(
        functools.partial(_merge_kernel, tm=tm),
        grid=(t // tm,),
        in_specs=[row(D_MODEL), row(D_MODEL), row(D_MODEL)] + grp(D_GROUP) + grp(LANES)
                 + [_const_spec((2, D_MODEL)), _const_spec((D_GROUP, D_MODEL)),
                    _const_spec((D_MODEL, D_MODEL))],
        out_specs=row(D_MODEL),
        out_shape=jax.ShapeDtypeStruct((t, D_MODEL), F32),
        scratch_shapes=scratch,
        compiler_params=pltpu.CompilerParams(dimension_semantics=("arbitrary",),
                                             vmem_limit_bytes=VMEM_LIMIT),
        name="merge",
    )(x2, yc, ga, *outs, *lses, gb, wao, wo)


def _pair_candidates():
    return [(a, b) for a in range(PEER_TOPK) for b in range(PEER_TOPK)
            if (a + 1) * (b + 1) <= PEER_TOPK]


def _rank_code(r):
    return -RANK_CODE_BASE * (1.0 + r / 32.0)


def _route_kernel(h_ref, nf_ref, wqt_ref, sk_ref,
                  u2q_ref, n_ref, r2_ref, e1_ref, e2_ref,
                  qt_ref, ts_ref, s_ref, rk_ref, *, tm, ql):
    u2 = _rms(h_ref[...], nf_ref[...])
    u2t = u2.T.astype(BF16)
    for qq in range(tm // ql):
        u2q_ref[qq] = u2t[:, qq * ql:(qq + 1) * ql]
    qt_ref[...] = jnp.dot(wqt_ref[...], u2t, preferred_element_type=F32).astype(BF16)
    n_lb = tm // LANES

    def head_body(h, carry):
        for c in range(2):
            r0 = pl.multiple_of(h * (2 * PEER_D_HALF) + c * PEER_D_HALF, PEER_D_HALF)
            s_ref[c, h] = jnp.dot(sk_ref[h, c], qt_ref[pl.ds(r0, PEER_D_HALF), :],
                                  preferred_element_type=F32)
            for lb in range(n_lb):
                ln = slice(lb * LANES, (lb + 1) * LANES)
                cur = s_ref[c, h, :, ln]
                for r in range(PEER_TOPK):
                    m = jnp.max(cur, axis=0, keepdims=True)
                    ts_ref[c, r, lb, pl.ds(h, 1), :] = m
                    cur = jnp.where(cur >= m, _rank_code(r), cur)
                rk_ref[c, h, :, ln] = cur
        return carry

    lax.fori_loop(0, PEER_HEADS, head_body, 0)

    def top_rows(c, r):
        return jnp.concatenate([ts_ref[c, r, lb] for lb in range(n_lb)], axis=1)

    top1 = [top_rows(0, r) for r in range(PEER_TOPK)]
    top2 = [top_rows(1, r) for r in range(PEER_TOPK)]
    pairs = _pair_candidates()
    cands = [top1[a] + top2[b] for a, b in pairs]
    cmax = top1[0] + top2[0]
    cur = list(cands)
    for r in range(PEER_TOPK):
        m = functools.reduce(jnp.maximum, cur)
        if r + 1 < PEER_TOPK:
            cur = [jnp.where(c >= m, -jnp.inf, c) for c in cur]
    tau = m
    sel = [c >= tau for c in cands]
    z = functools.reduce(jnp.add, [jnp.where(s, jnp.exp(c - cmax), 0.0)
                                   for s, c in zip(sel, cands)])
    half_inv_z = 0.5 / z
    cnt = []
    for a in range(PEER_TOPK):
        cnt.append(functools.reduce(
            jnp.add, [jnp.where(s, 1.0, 0.0) for s, (pa, _) in zip(sel, pairs) if pa == a]))

    for h in range(PEER_HEADS):
        row = slice(h, h + 1)
        for lb in range(n_lb):
            ln = slice(lb * LANES, (lb + 1) * LANES)
            code1 = rk_ref[0, h, :, ln]
            n = jnp.zeros(code1.shape, F32)
            for a in range(PEER_TOPK):
                n = jnp.where(code1 == _rank_code(a), cnt[a][row, ln], n)
            n_ref[h, :, ln] = n
            e1_ref[h, :, ln] = jnp.exp(s_ref[0, h, :, ln] - top1[0][row, ln]) * half_inv_z[row, ln]
            code2 = rk_ref[1, h, :, ln]
            rank2 = jnp.where(code2 <= -RANK_CODE_BASE,
                              (code2 * (-1.0 / RANK_CODE_BASE) - 1.0) * 32.0, float(PEER_TOPK))
            r2_ref[h, :, ln] = rank2.astype(BF16)
            e2_ref[h, :, ln] = jnp.exp(s_ref[1, h, :, ln] - top2[0][row, ln]).astype(BF16)


def _route_call(h1, nf, wqt, sk):
    t = h1.shape[0]
    tm, ql = TM_ROUTE, QL_DENSE
    big = pl.BlockSpec((PEER_HEADS, PEER_N_KEYS, tm), lambda i: (0, 0, i))
    shape = lambda dt: jax.ShapeDtypeStruct((PEER_HEADS, PEER_N_KEYS, t), dt)
    return pl.pallas_call(
        functools.partial(_route_kernel, tm=tm, ql=ql),
        grid=(t // tm,),
        in_specs=[pl.BlockSpec((tm, D_MODEL), lambda i: (i, 0)), _const_spec((1, D_MODEL)),
                  _const_spec(wqt.shape), _const_spec(sk.shape)],
        out_specs=[pl.BlockSpec((tm // ql, D_MODEL, ql), lambda i: (i, 0, 0)),
                   big, big, big, big],
        out_shape=[jax.ShapeDtypeStruct((t // ql, D_MODEL, ql), BF16), shape(F32), shape(BF16),
                   shape(F32), shape(BF16)],
        scratch_shapes=[pltpu.VMEM((wqt.shape[0], tm), BF16),
                        pltpu.VMEM((2, PEER_TOPK, tm // LANES, PEER_HEADS, LANES), F32),
                        pltpu.VMEM((2, PEER_HEADS, PEER_N_KEYS, tm), F32),
                        pltpu.VMEM((2, PEER_HEADS, PEER_N_KEYS, tm), F32)],
        compiler_params=pltpu.CompilerParams(dimension_semantics=("arbitrary",),
                                             vmem_limit_bytes=VMEM_LIMIT),
        name="route",
    )(h1, nf, wqt, sk)


def _dense_kernel(u2q_ref, down_ref, down_next_ref, upt_ref, upt_prev_ref,
                  n_ref, r2_ref, e1_ref, e2_ref, h_ref, fn_ref, out_ref,
                  acc_ref, ht_ref, wt_ref, *, tm, te, ql):
    j = pl.program_id(1)
    last_j = pl.num_programs(1) - 1
    nc = te // PEER_N_KEYS
    nq = tm // ql

    def hidden(w_ref, q):
        ht_ref[q] = jnp.dot(w_ref[...], u2q_ref[q], preferred_element_type=F32)

    def project(w_ref, q):
        acc_ref[:, q * ql:(q + 1) * ql] += jnp.dot(w_ref[...], wt_ref[q],
                                                   preferred_element_type=F32)

    def key_rows(ref, h, cc, ln):
        tile = jnp.broadcast_to(ref[h, cc:cc + 1, ln], (BF16_ROWS, ql)).astype(BF16)
        return jnp.concatenate([tile] * (PEER_N_KEYS // BF16_ROWS), axis=0)

    @pl.when(j == 0)
    def _():
        acc_ref[...] = jnp.zeros_like(acc_ref)
        wt_ref[nq - 1] = jnp.zeros((te, ql), BF16)
        hidden(down_ref, 0)

    for q in range(nq):
        if q + 1 < nq:
            hidden(down_ref, q + 1)
        else:
            hidden(down_next_ref, 0)
        ln = slice(q * ql, (q + 1) * ql)
        for cc in range(nc):
            rows = slice(cc * PEER_N_KEYS, (cc + 1) * PEER_N_KEYS)
            g = jnp.zeros((PEER_N_KEYS, ql), BF16)
            for h in range(PEER_HEADS):
                gate = e2_ref[h, :, ln] * key_rows(e1_ref, h, cc, ln)
                g = g + jnp.where(r2_ref[h, :, ln] < key_rows(n_ref, h, cc, ln), gate,
                                  jnp.zeros((), BF16))
            hv = ht_ref[q, rows, :]
            act = hv * (1.0 + lax.erf(hv * np.float32(2.0 ** -0.5)))
            wt_ref[q, rows, :] = g * act.astype(BF16)
        if q == 0:
            project(upt_prev_ref, nq - 1)
        if q + 1 < nq:
            project(upt_ref, q)

    @pl.when(j == last_j)
    def _():
        project(upt_ref, nq - 1)
        out_ref[...] = _rms(h_ref[...] + acc_ref[...].T, fn_ref[...])


def _dense_call(u2q, down, upt, n, r2, e1, e2, h1, fn):
    t = h1.shape[0]
    n_exp = down.shape[0]
    tm, te, ql = TM_DENSE, TE_DENSE, QL_DENSE
    nc = te // PEER_N_KEYS
    nj = n_exp // te
    big = pl.BlockSpec((PEER_HEADS, PEER_N_KEYS, tm), lambda i, j: (0, 0, i))
    rows = pl.BlockSpec((PEER_HEADS, nc, tm), lambda i, j: (0, j, i))
    return pl.pallas_call(
        functools.partial(_dense_kernel, tm=tm, te=te, ql=ql),
        grid=(t // tm, nj),
        in_specs=[pl.BlockSpec((tm // ql, D_MODEL, ql), lambda i, j: (i, 0, 0)),
                  pl.BlockSpec((te, D_MODEL), lambda i, j: (j, 0)),
                  pl.BlockSpec((te, D_MODEL), lambda i, j: (jnp.minimum(j + 1, nj - 1), 0)),
                  pl.BlockSpec((D_MODEL, te), lambda i, j: (0, j)),
                  pl.BlockSpec((D_MODEL, te), lambda i, j: (0, jnp.maximum(j - 1, 0))),
                  rows, big, rows, big,
                  pl.BlockSpec((tm, D_MODEL), lambda i, j: (i, 0)),
                  pl.BlockSpec((1, D_MODEL), lambda i, j: (0, 0))],
        out_specs=pl.BlockSpec((tm, D_MODEL), lambda i, j: (i, 0)),
        out_shape=jax.ShapeDtypeStruct((t, D_MODEL), F32),
        scratch_shapes=[pltpu.VMEM((D_MODEL, tm), F32), pltpu.VMEM((tm // ql, te, ql), F32),
                        pltpu.VMEM((tm // ql, te, ql), BF16)],
        compiler_params=pltpu.CompilerParams(dimension_semantics=("arbitrary", "arbitrary"),
                                             vmem_limit_bytes=VMEM_LIMIT),
        name="dense",
    )(u2q, down, down, upt, upt, n, r2, e1, e2, h1, fn)


def _rope_lane_freqs():
    inv_freq = ROPE_THETA ** (-jnp.arange(ROT_HALF, dtype=F32) * (2.0 / ROT_DIM))
    lane = np.arange(LANES) % HEAD_DIM
    return jnp.where(lane < ROT_DIM, inv_freq[lane % ROT_HALF], 0.0).reshape(1, LANES).astype(F32)


def kernel(x, positions, norm_mix, w_in, conv_w, w_conv_out, w_attn_out, gate_bias, w_out,
           norm_ffn, peer_w_query, peer_sub_keys, peer_down, peer_up, final_norm):
    batch, seq, d = x.shape
    assert d == D_MODEL and w_in.shape[-1] == D_IN
    depth = w_in.shape[0]
    t = batch * seq
    h = x.reshape(t, d)
    pos2 = positions.reshape(t, 1)
    invf = _rope_lane_freqs()
    for layer in range(depth):
        yc, ga, qs, ks, vs = _proj_call(h, pos2, norm_mix[layer].reshape(1, d), invf,
                                        w_in[layer].astype(BF16),
                                        conv_w[layer], gate_bias[layer],
                                        w_conv_out[layer].astype(BF16), seq=seq)
        outs, lses = zip(*[_attn_call(qs[g], ks[g], vs[g], g, batch=batch, seq=seq)
                           for g in range(N_GROUPS)])
        h1 = _merge_call(h, yc, ga, outs, lses, gate_bias[layer],
                         w_attn_out[layer].astype(BF16), w_out[layer].astype(BF16))
        u2q, n, r2, e1, e2 = _route_call(
            h1, norm_ffn[layer].reshape(1, d), peer_w_query[layer].T.astype(BF16),
            peer_sub_keys[layer].astype(BF16))
        assert layer == depth - 1, "intermediate layers need a dense call without the final norm"
        h = _dense_call(u2q, peer_down[layer].astype(BF16), peer_up[layer].T.astype(BF16),
                        n, r2, e1, e2, h1, final_norm.reshape(1, d))
    return h.reshape(batch, seq, d)
```

```python
import functools

import numpy as np
import jax
import jax.numpy as jnp
from jax import lax
from jax.experimental import pallas as pl
from jax.experimental.pallas import tpu as pltpu

F32 = jnp.float32
BF16 = jnp.bfloat16

D_MODEL = 1024
HEAD_DIM = 64
HEADS_PER_GROUP = 8
DILATED_GROUPS = ((128, 1), (512, 4), (2048, 16))
N_GROUPS = len(DILATED_GROUPS)
D_GROUP = HEADS_PER_GROUP * HEAD_DIM
D_ATTN = N_GROUPS * D_GROUP
BLOCK = 128
ROT_DIM = HEAD_DIM // 4
ROT_HALF = ROT_DIM // 2
ROPE_THETA = 500000.0
PEER_HEADS = 8
PEER_N_KEYS = 128
PEER_TOPK = 16
PEER_D_HALF = 128
RMS_EPS = 1e-6
LANES = 128
SUBLANES = 8
BF16_ROWS = 2 * SUBLANES
RANK_CODE_BASE = 2.0 ** 120
VMEM_LIMIT = 60 * 1024 * 1024

COL_B, COL_C, COL_X, COL_QKV = 0, D_MODEL, 2 * D_MODEL, 3 * D_MODEL
COL_GC = COL_QKV + 3 * D_ATTN
COL_GA = COL_GC + D_MODEL
D_IN = COL_GA + D_MODEL

LSE_LANES = LANES // HEADS_PER_GROUP
ATTN_BLOCKS_PER_STEP = 4
TM_PROJ = 512
TM_MERGE = 512
TM_ROUTE = 512
TM_DENSE = 1024
TE_DENSE = SUBLANES * PEER_N_KEYS
QL_DENSE = 256


def _const_spec(shape):
    nd = len(shape)
    return pl.BlockSpec(shape, lambda *_: (0,) * nd, pipeline_mode=pl.Buffered(1))


def _rms(x, g):
    return x * lax.rsqrt(jnp.mean(x * x, axis=-1, keepdims=True) + RMS_EPS) * g


def _proj_kernel(x_ref, pos_ref, nm_ref, invf_ref, w_ref, cw_ref, gb_ref, wco_ref, *rest,
                 tm, seq):
    yc_ref, ga_ref = rest[0], rest[1]
    qkv_refs = rest[2:2 + 3 * N_GROUPS]
    zbuf_ref, t_ref = rest[2 + 3 * N_GROUPS:]
    i = pl.program_id(0)
    u = _rms(x_ref[...], nm_ref[...]).astype(BF16)

    def mm(c0, n):
        return jnp.dot(u, w_ref[:, c0:c0 + n], preferred_element_type=F32)

    @pl.when((i * tm) % seq == 0)
    def _():
        zbuf_ref[0:8, :] = jnp.zeros((8, D_MODEL), F32)

    z = mm(COL_C, D_MODEL) * mm(COL_X, D_MODEL)
    zbuf_ref[8:tm + 8, :] = z
    cw = cw_ref[...]
    conv = cw[0:1] * zbuf_ref[6:tm + 6, :] + cw[1:2] * zbuf_ref[7:tm + 7, :] + cw[2:3] * z
    zbuf_ref[0:8, :] = zbuf_ref[tm:tm + 8, :]
    yb = (mm(COL_B, D_MODEL) * conv).astype(BF16)
    yc = jnp.dot(yb, wco_ref[...], preferred_element_type=F32)
    yc_ref[...] = (jax.nn.sigmoid(mm(COL_GC, D_MODEL) + gb_ref[0:1, :]) * yc).astype(BF16)
    ga_ref[...] = mm(COL_GA, D_MODEL).astype(BF16)

    ang = pos_ref[...].astype(F32) * invf_ref[...]
    cosv = jnp.cos(ang)
    sinv = jnp.sin(ang)
    lane = lax.broadcasted_iota(jnp.int32, (1, LANES), 1) % HEAD_DIM
    sin_lo = jnp.where(lane < ROT_HALF, -sinv, 0.0)
    sin_hi = jnp.where((lane >= ROT_HALF) & (lane < ROT_DIM), sinv, 0.0)
    for blk in range(3 * N_GROUPS):
        t = mm(COL_QKV + blk * D_GROUP, D_GROUP)
        dil = DILATED_GROUPS[blk % N_GROUPS][1]
        out_ref = qkv_refs[blk]
        for s in range(D_GROUP // LANES):
            ts = t[:, s * LANES:(s + 1) * LANES]
            if blk < 2 * N_GROUPS:
                ts = (ts * cosv + pltpu.roll(ts, LANES - ROT_HALF, 1) * sin_lo
                      + pltpu.roll(ts, ROT_HALF, 1) * sin_hi)
            if dil == 1:
                out_ref[0, :, s * LANES:(s + 1) * LANES] = ts.astype(BF16)
            else:
                t_ref[s] = ts
        if dil > 1:
            for r in range(dil):
                for s in range(D_GROUP // LANES):
                    out_ref[r, :, s * LANES:(s + 1) * LANES] = t_ref[
                        s, pl.ds(r, tm // dil, stride=dil), :].astype(BF16)


def _proj_call(x2, pos2, nm, invf, w_p, cw, gb, wco, *, seq):
    t = x2.shape[0]
    tm = TM_PROJ
    row = lambda n: pl.BlockSpec((tm, n), lambda i: (i, 0))
    dils = [d for _, d in DILATED_GROUPS] * 3
    qkv_specs = [pl.BlockSpec((d, tm // d, D_GROUP), lambda i: (0, i, 0)) for d in dils]
    qkv_shapes = [jax.ShapeDtypeStruct((d, t // d, D_GROUP), BF16) for d in dils]
    res = pl.pallas_call(
        functools.partial(_proj_kernel, tm=tm, seq=seq),
        grid=(t // tm,),
        in_specs=[row(D_MODEL), row(1), _const_spec((1, D_MODEL)), _const_spec((1, LANES)),
                  _const_spec((D_MODEL, D_IN)), _const_spec((3, D_MODEL)),
                  _const_spec((2, D_MODEL)), _const_spec((D_MODEL, D_MODEL))],
        out_specs=[row(D_MODEL), row(D_MODEL)] + qkv_specs,
        out_shape=[jax.ShapeDtypeStruct((t, D_MODEL), BF16),
                   jax.ShapeDtypeStruct((t, D_MODEL), BF16)] + qkv_shapes,
        scratch_shapes=[pltpu.VMEM((tm + 8, D_MODEL), F32),
                        pltpu.VMEM((D_GROUP // LANES, tm, LANES), F32)],
        compiler_params=pltpu.CompilerParams(dimension_semantics=("arbitrary",),
                                             vmem_limit_bytes=VMEM_LIMIT),
        name="proj",
    )(x2, pos2, nm, invf, w_p, cw, gb, wco)
    yc, ga = res[0], res[1]
    qs, ks, vs = res[2:2 + N_GROUPS], res[2 + N_GROUPS:2 + 2 * N_GROUPS], res[2 + 2 * N_GROUPS:]
    return yc, ga, qs, ks, vs


def _attn_kernel(q_ref, kp_ref, kc_ref, vp_ref, vc_ref, o_ref, lse_ref, *, steps, nsub):
    n = pl.program_id(2)
    qi = lax.broadcasted_iota(jnp.int32, (BLOCK, 2 * BLOCK), 0)
    ki = lax.broadcasted_iota(jnp.int32, (BLOCK, 2 * BLOCK), 1)
    dist = BLOCK + qi - ki
    band = (dist >= 0) & (dist <= steps)
    valid_first = band & ((n > 0) | (ki >= BLOCK))
    first = lax.broadcasted_iota(jnp.int32, (1, LANES), 1) < HEAD_DIM
    ones = jnp.ones((2 * BLOCK, LANES), BF16)
    scale = HEAD_DIM ** -0.5
    n_slab = D_GROUP // LANES
    slabs = [slice(hp * LANES, (hp + 1) * LANES) for hp in range(n_slab)]
    picks = (first, jnp.logical_not(first))

    def keys_of(prev_ref, cur_ref, sub, sl):
        if sub == 0:
            return jnp.concatenate([prev_ref[:, sl], cur_ref[0:BLOCK, sl]], axis=0)
        return cur_ref[(sub - 1) * BLOCK:(sub + 1) * BLOCK, sl]

    units = [(sub, hp) for sub in range(nsub) for hp in range(n_slab)]
    scores = []
    for sub, hp in units:
        sl = slabs[hp]
        q2 = q_ref[sub * BLOCK:(sub + 1) * BLOCK, sl] * scale
        k2 = keys_of(kp_ref, kc_ref, sub, sl)
        for pick in picks:
            qh = jnp.where(pick, q2, jnp.zeros((), BF16))
            s = lax.dot_general(qh, k2, (((1,), (1,)), ((), ())), preferred_element_type=F32)
            scores.append(jnp.where(valid_first if sub == 0 else band, s, -jnp.inf))
    maxes = [jnp.max(s, axis=-1, keepdims=True) for s in scores]
    probs = [jnp.exp(s - m).astype(BF16) for s, m in zip(scores, maxes)]
    head_of_lane = lax.broadcasted_iota(jnp.int32, (1, LANES), 1) // LSE_LANES
    lse = [None] * nsub
    for u, (sub, hp) in enumerate(units):
        sl = slabs[hp]
        rows = slice(sub * BLOCK, (sub + 1) * BLOCK)
        v2 = keys_of(vp_ref, vc_ref, sub, sl)
        pa, pb = probs[2 * u], probs[2 * u + 1]
        la = jnp.dot(pa, ones, preferred_element_type=F32)
        lb = jnp.dot(pb, ones, preferred_element_type=F32)
        o = jnp.where(first, jnp.dot(pa, v2, preferred_element_type=F32),
                      jnp.dot(pb, v2, preferred_element_type=F32))
        o_ref[rows, sl] = (o / jnp.where(first, la, lb)).astype(BF16)
        for k, (m, l) in enumerate(((maxes[2 * u], la), (maxes[2 * u + 1], lb))):
            val = m + jnp.log(l)
            lse[sub] = val if lse[sub] is None else jnp.where(
                head_of_lane == 2 * hp + k, val, lse[sub])
    for sub in range(nsub):
        lse_ref[sub * BLOCK:(sub + 1) * BLOCK, :] = lse[sub]


def _attn_call(q, k, v, g, *, batch, seq):
    window, dil = DILATED_GROUPS[g]
    steps = window // dil
    nsub = ATTN_BLOCKS_PER_STEP
    assert steps <= BLOCK and seq % (dil * BLOCK * nsub) == 0
    nb = seq // (dil * BLOCK)
    cur = pl.BlockSpec((None, nsub * BLOCK, D_GROUP), lambda b, r, n: (r, b * (nb // nsub) + n, 0))
    lse_spec = pl.BlockSpec((None, nsub * BLOCK, LANES), lambda b, r, n: (r, b * (nb // nsub) + n, 0))
    prev = pl.BlockSpec((None, BLOCK, D_GROUP),
                        lambda b, r, n: (r, b * nb + jnp.maximum(nsub * n - 1, 0), 0))
    return pl.pallas_call(
        functools.partial(_attn_kernel, steps=steps, nsub=nsub),
        grid=(batch, dil, nb // nsub),
        in_specs=[cur, prev, cur, prev, cur],
        out_specs=[cur, lse_spec],
        out_shape=[jax.ShapeDtypeStruct(q.shape, BF16),
                   jax.ShapeDtypeStruct(q.shape[:2] + (LANES,), F32)],
        compiler_params=pltpu.CompilerParams(
            dimension_semantics=("arbitrary", "arbitrary", "arbitrary")),
        name=f"attn{g}",
    )(q, k, k, v, v)


def _merge_kernel(x_ref, yc_ref, ga_ref, o0_ref, o1_ref, o2_ref, l0_ref, l1_ref, l2_ref,
                  gb_ref, wao_ref, wo_ref, h_ref, *scratch, tm):
    def natural(ref, g, scr, width):
        dil = DILATED_GROUPS[g][1]
        if dil == 1:
            return ref[0].astype(F32)
        n_slab = width // LANES
        for r in range(dil):
            for s in range(n_slab):
                scr[s, pl.ds(r, tm // dil, stride=dil), :] = ref[
                    r, :, s * LANES:(s + 1) * LANES].astype(F32)
        return jnp.concatenate([scr[s] for s in range(n_slab)], axis=1)

    scr = iter(scratch)
    o_refs, l_refs = (o0_ref, o1_ref, o2_ref), (l0_ref, l1_ref, l2_ref)
    dilated = [DILATED_GROUPS[g][1] > 1 for g in range(N_GROUPS)]
    ls = [natural(l_refs[g], g, next(scr) if dilated[g] else None, LANES)
          for g in range(N_GROUPS)]
    os_ = [natural(o_refs[g], g, next(scr) if dilated[g] else None, D_GROUP)
           for g in range(N_GROUPS)]
    mx = functools.reduce(jnp.maximum, ls)
    es = [jnp.exp(l - mx) for l in ls]
    inv = 1.0 / sum(es)
    row = lax.broadcasted_iota(jnp.int32, (LANES, D_GROUP), 0)
    col = lax.broadcasted_iota(jnp.int32, (LANES, D_GROUP), 1)
    spread = jnp.where(row == (col // HEAD_DIM) * LSE_LANES, 1.0, 0.0).astype(BF16)
    o = sum(jnp.dot((e * inv).astype(BF16), spread, preferred_element_type=F32) * ov
            for e, ov in zip(es, os_))
    ya = jnp.dot(o.astype(BF16), wao_ref[...], preferred_element_type=F32)
    merged = (yc_ref[...].astype(F32)
              + jax.nn.sigmoid(ga_ref[...].astype(F32) + gb_ref[1:2, :]) * ya)
    h_ref[...] = x_ref[...] + jnp.dot(merged.astype(BF16), wo_ref[...],
                                      preferred_element_type=F32)


def _merge_call(x2, yc, ga, outs, lses, gb, wao, wo):
    t = x2.shape[0]
    tm = TM_MERGE
    row = lambda n: pl.BlockSpec((tm, n), lambda i: (i, 0))
    dils = [d for _, d in DILATED_GROUPS]
    grp = lambda w: [pl.BlockSpec((d, tm // d, w), lambda i: (0, i, 0)) for d in dils]
    scratch = ([pltpu.VMEM((1, tm, LANES), F32) for d in dils if d > 1]
               + [pltpu.VMEM((D_GROUP // LANES, tm, LANES), F32) for d in dils if d > 1])
    return pl.pallas_call(
        functools.partial(_merge_kernel, tm=tm),
        grid=(t // tm,),
        in_specs=[row(D_MODEL), row(D_MODEL), row(D_MODEL)] + grp(D_GROUP) + grp(LANES)
                 + [_const_spec((2, D_MODEL)), _const_spec((D_GROUP, D_MODEL)),
                    _const_spec((D_MODEL, D_MODEL))],
        out_specs=row(D_MODEL),
        out_shape=jax.ShapeDtypeStruct((t, D_MODEL), F32),
        scratch_shapes=scratch,
        compiler_params=pltpu.CompilerParams(dimension_semantics=("arbitrary",),
                                             vmem_limit_bytes=VMEM_LIMIT),
        name="merge",
    )(x2, yc, ga, *outs, *lses, gb, wao, wo)


def _pair_candidates():
    return [(a, b) for a in range(PEER_TOPK) for b in range(PEER_TOPK)
            if (a + 1) * (b + 1) <= PEER_TOPK]


def _rank_code(r):
    return -RANK_CODE_BASE * (1.0 + r / 32.0)


def _route_kernel(h_ref, nf_ref, wqt_ref, sk_ref,
                  u2q_ref, n_ref, r2_ref, e1_ref, e2_ref,
                  qt_ref, ts_ref, s_ref, rk_ref, *, tm, ql):
    u2 = _rms(h_ref[...], nf_ref[...])
    u2t = u2.T.astype(BF16)
    for qq in range(tm // ql):
        u2q_ref[qq] = u2t[:, qq * ql:(qq + 1) * ql]
    qt_ref[...] = jnp.dot(wqt_ref[...], u2t, preferred_element_type=F32).astype(BF16)
    n_lb = tm // LANES

    def head_body(h, carry):
        for c in range(2):
            r0 = pl.multiple_of(h * (2 * PEER_D_HALF) + c * PEER_D_HALF, PEER_D_HALF)
            s_ref[c, h] = jnp.dot(sk_ref[h, c], qt_ref[pl.ds(r0, PEER_D_HALF), :],
                                  preferred_element_type=F32)
            for lb in range(n_lb):
                ln = slice(lb * LANES, (lb + 1) * LANES)
                cur = s_ref[c, h, :, ln]
                for r in range(PEER_TOPK):
                    m = jnp.max(cur, axis=0, keepdims=True)
                    ts_ref[c, r, lb, pl.ds(h, 1), :] = m
                    cur = jnp.where(cur >= m, _rank_code(r), cur)
                rk_ref[c, h, :, ln] = cur
        return carry

    lax.fori_loop(0, PEER_HEADS, head_body, 0)

    def top_rows(c, r):
        return jnp.concatenate([ts_ref[c, r, lb] for lb in range(n_lb)], axis=1)

    top1 = [top_rows(0, r) for r in range(PEER_TOPK)]
    top2 = [top_rows(1, r) for r in range(PEER_TOPK)]
    pairs = _pair_candidates()
    cands = [top1[a] + top2[b] for a, b in pairs]
    cmax = top1[0] + top2[0]
    cur = list(cands)
    for r in range(PEER_TOPK):
        m = functools.reduce(jnp.maximum, cur)
        if r + 1 < PEER_TOPK:
            cur = [jnp.where(c >= m, -jnp.inf, c) for c in cur]
    tau = m
    sel = [c >= tau for c in cands]
    z = functools.reduce(jnp.add, [jnp.where(s, jnp.exp(c - cmax), 0.0)
                                   for s, c in zip(sel, cands)])
    half_inv_z = 0.5 / z
    cnt = []
    for a in range(PEER_TOPK):
        cnt.append(functools.reduce(
            jnp.add, [jnp.where(s, 1.0, 0.0) for s, (pa, _) in zip(sel, pairs) if pa == a]))

    for h in range(PEER_HEADS):
        row = slice(h, h + 1)
        for lb in range(n_lb):
            ln = slice(lb * LANES, (lb + 1) * LANES)
            code1 = rk_ref[0, h, :, ln]
            n = jnp.zeros(code1.shape, F32)
            for a in range(PEER_TOPK):
                n = jnp.where(code1 == _rank_code(a), cnt[a][row, ln], n)
            n_ref[h, :, ln] = n
            e1_ref[h, :, ln] = jnp.exp(s_ref[0, h, :, ln] - top1[0][row, ln]) * half_inv_z[row, ln]
            code2 = rk_ref[1, h, :, ln]
            rank2 = jnp.where(code2 <= -RANK_CODE_BASE,
                              (code2 * (-1.0 / RANK_CODE_BASE) - 1.0) * 32.0, float(PEER_TOPK))
            r2_ref[h, :, ln] = rank2.astype(BF16)
            e2_ref[h, :, ln] = jnp.exp(s_ref[1, h, :, ln] - top2[0][row, ln]).astype(BF16)


def _route_call(h1, nf, wqt, sk):
    t = h1.shape[0]
    tm, ql = TM_ROUTE, QL_DENSE
    big = pl.BlockSpec((PEER_HEADS, PEER_N_KEYS, tm), lambda i: (0, 0, i))
    shape = lambda dt: jax.ShapeDtypeStruct((PEER_HEADS, PEER_N_KEYS, t), dt)
    return pl.pallas_call(
        functools.partial(_route_kernel, tm=tm, ql=ql),
        grid=(t // tm,),
        in_specs=[pl.BlockSpec((tm, D_MODEL), lambda i: (i, 0)), _const_spec((1, D_MODEL)),
                  _const_spec(wqt.shape), _const_spec(sk.shape)],
        out_specs=[pl.BlockSpec((tm // ql, D_MODEL, ql), lambda i: (i, 0, 0)),
                   big, big, big, big],
        out_shape=[jax.ShapeDtypeStruct((t // ql, D_MODEL, ql), BF16), shape(F32), shape(BF16),
                   shape(F32), shape(BF16)],
        scratch_shapes=[pltpu.VMEM((wqt.shape[0], tm), BF16),
                        pltpu.VMEM((2, PEER_TOPK, tm // LANES, PEER_HEADS, LANES), F32),
                        pltpu.VMEM((2, PEER_HEADS, PEER_N_KEYS, tm), F32),
                        pltpu.VMEM((2, PEER_HEADS, PEER_N_KEYS, tm), F32)],
        compiler_params=pltpu.CompilerParams(dimension_semantics=("arbitrary",),
                                             vmem_limit_bytes=VMEM_LIMIT),
        name="route",
    )(h1, nf, wqt, sk)


def _dense_kernel(u2q_ref, down_next_ref, down_first_ref, upt_ref,
                  n_ref, r2_ref, e1_ref, e2_ref, h_ref, fn_ref, out_ref,
                  acc_ref, ht_ref, wt_ref, *, tm, te, ql):
    i, j = pl.program_id(0), pl.program_id(1)
    last_j = pl.num_programs(1) - 1
    nc = te // PEER_N_KEYS
    nq = tm // ql

    def hidden(w_ref, q):
        ht_ref[q] = jnp.dot(w_ref[...], u2q_ref[q], preferred_element_type=F32)

    def project(w_ref, q):
        acc_ref[:, q * ql:(q + 1) * ql] += jnp.dot(w_ref[...], wt_ref[q],
                                                   preferred_element_type=F32)

    def key_rows(ref, h, cc, ln):
        tile = jnp.broadcast_to(ref[h, cc:cc + 1, ln], (BF16_ROWS, ql)).astype(BF16)
        return jnp.concatenate([tile] * (PEER_N_KEYS // BF16_ROWS), axis=0)

    @pl.when((i == 0) & (j == 0))
    def _():
        for q in range(nq):
            hidden(down_first_ref, q)

    @pl.when(j == 0)
    def _():
        acc_ref[...] = jnp.zeros_like(acc_ref)

    for q in range(nq):
        ln = slice(q * ql, (q + 1) * ql)
        for cc in range(nc):
            rows = slice(cc * PEER_N_KEYS, (cc + 1) * PEER_N_KEYS)
            g = jnp.zeros((PEER_N_KEYS, ql), BF16)
            for h in range(PEER_HEADS):
                gate = e2_ref[h, :, ln] * key_rows(e1_ref, h, cc, ln)
                g = g + jnp.where(r2_ref[h, :, ln] < key_rows(n_ref, h, cc, ln), gate,
                                  jnp.zeros((), BF16))
            hv = ht_ref[q, rows, :]
            act = hv * (1.0 + lax.erf(hv * np.float32(2.0 ** -0.5)))
            wt_ref[q, rows, :] = g * act.astype(BF16)
        project(upt_ref, q)
        hidden(down_next_ref, q)

    @pl.when(j == last_j)
    def _():
        out_ref[...] = _rms(h_ref[...] + acc_ref[...].T, fn_ref[...])


def _dense_call(u2q, down, upt, n, r2, e1, e2, h1, fn):
    t = h1.shape[0]
    n_exp = down.shape[0]
    tm, te, ql = TM_DENSE, TE_DENSE, QL_DENSE
    nc = te // PEER_N_KEYS
    ni, nj = t // tm, n_exp // te
    once = pl.Buffered(1)
    big = pl.BlockSpec((PEER_HEADS, PEER_N_KEYS, tm), lambda i, j: (0, 0, i))
    rows = pl.BlockSpec((PEER_HEADS, nc, tm), lambda i, j: (0, j, i))
    return pl.pallas_call(
        functools.partial(_dense_kernel, tm=tm, te=te, ql=ql),
        grid=(ni, nj),
        in_specs=[
            pl.BlockSpec((tm // ql, D_MODEL, ql),
                         lambda i, j: (jnp.minimum(i + (j + 1) // nj, ni - 1), 0, 0)),
            pl.BlockSpec((te, D_MODEL), lambda i, j: ((j + 1) % nj, 0)),
            pl.BlockSpec((te, D_MODEL), lambda i, j: (0, 0), pipeline_mode=once),
            pl.BlockSpec((D_MODEL, te), lambda i, j: (0, j)),
            rows, big, rows, big,
            pl.BlockSpec((tm, D_MODEL), lambda i, j: (i, 0)),
            pl.BlockSpec((1, D_MODEL), lambda i, j: (0, 0))],
        out_specs=pl.BlockSpec((tm, D_MODEL), lambda i, j: (i, 0)),
        out_shape=jax.ShapeDtypeStruct((t, D_MODEL), F32),
        scratch_shapes=[pltpu.VMEM((D_MODEL, tm), F32), pltpu.VMEM((tm // ql, te, ql), F32),
                        pltpu.VMEM((tm // ql, te, ql), BF16)],
        compiler_params=pltpu.CompilerParams(dimension_semantics=("arbitrary", "arbitrary"),
                                             vmem_limit_bytes=VMEM_LIMIT),
        name="dense",
    )(u2q, down, down, upt, n, r2, e1, e2, h1, fn)


def _rope_lane_freqs():
    inv_freq = ROPE_THETA ** (-jnp.arange(ROT_HALF, dtype=F32) * (2.0 / ROT_DIM))
    lane = np.arange(LANES) % HEAD_DIM
    return jnp.where(lane < ROT_DIM, inv_freq[lane % ROT_HALF], 0.0).reshape(1, LANES).astype(F32)


def kernel(x, positions, norm_mix, w_in, conv_w, w_conv_out, w_attn_out, gate_bias, w_out,
           norm_ffn, peer_w_query, peer_sub_keys, peer_down, peer_up, final_norm):
    batch, seq, d = x.shape
    assert d == D_MODEL and w_in.shape[-1] == D_IN
    depth = w_in.shape[0]
    t = batch * seq
    h = x.reshape(t, d)
    pos2 = positions.reshape(t, 1)
    invf = _rope_lane_freqs()
    for layer in range(depth):
        yc, ga, qs, ks, vs = _proj_call(h, pos2, norm_mix[layer].reshape(1, d), invf,
                                        w_in[layer].astype(BF16),
                                        conv_w[layer], gate_bias[layer],
                                        w_conv_out[layer].astype(BF16), seq=seq)
        outs, lses = zip(*[_attn_call(qs[g], ks[g], vs[g], g, batch=batch, seq=seq)
                           for g in range(N_GROUPS)])
        h1 = _merge_call(h, yc, ga, outs, lses, gate_bias[layer],
                         w_attn_out[layer].astype(BF16), w_out[layer].astype(BF16))
        u2q, n, r2, e1, e2 = _route_call(
            h1, norm_ffn[layer].reshape(1, d), peer_w_query[layer].T.astype(BF16),
            peer_sub_keys[layer].astype(BF16))
        assert layer == depth - 1, "intermediate layers need a dense call without the final norm"
        h = _dense_call(u2q, peer_down[layer].astype(BF16), peer_up[layer].T.astype(BF16),
                        n, r2, e1, e2, h1, final_norm.reshape(1, d))
    return h.reshape(batch, seq, d)
```

```python
import functools

import numpy as np
import jax
import jax.numpy as jnp
from jax import lax
from jax.experimental import pallas as pl
from jax.experimental.pallas import tpu as pltpu

F32 = jnp.float32
BF16 = jnp.bfloat16

D_MODEL = 1024
HEAD_DIM = 64
HEADS_PER_GROUP = 8
DILATED_GROUPS = ((128, 1), (512, 4), (2048, 16))
N_GROUPS = len(DILATED_GROUPS)
D_GROUP = HEADS_PER_GROUP * HEAD_DIM
D_ATTN = N_GROUPS * D_GROUP
BLOCK = 128
ROT_DIM = HEAD_DIM // 4
ROT_HALF = ROT_DIM // 2
ROPE_THETA = 500000.0
PEER_HEADS = 8
PEER_N_KEYS = 128
PEER_TOPK = 16
PEER_D_HALF = 128
RMS_EPS = 1e-6
LANES = 128
SUBLANES = 8
BF16_ROWS = 2 * SUBLANES
RANK_CODE_BASE = 2.0 ** 120
VMEM_LIMIT = 60 * 1024 * 1024

COL_B, COL_C, COL_X, COL_QKV = 0, D_MODEL, 2 * D_MODEL, 3 * D_MODEL
COL_GC = COL_QKV + 3 * D_ATTN
COL_GA = COL_GC + D_MODEL
D_IN = COL_GA + D_MODEL

LSE_LANES = LANES // HEADS_PER_GROUP
ATTN_BLOCKS_PER_STEP = 4
TM_PROJ = 512
TM_MERGE = 512
TM_ROUTE = 512
TM_DENSE = 512
TE_DENSE = SUBLANES * PEER_N_KEYS
QL_DENSE = 256


def _const_spec(shape):
    nd = len(shape)
    return pl.BlockSpec(shape, lambda *_: (0,) * nd, pipeline_mode=pl.Buffered(1))


def _rms(x, g):
    return x * lax.rsqrt(jnp.mean(x * x, axis=-1, keepdims=True) + RMS_EPS) * g


def _proj_kernel(x_ref, pos_ref, nm_ref, invf_ref, w_ref, cw_ref, gb_ref, wco_ref, *rest,
                 tm, seq):
    yc_ref, ga_ref = rest[0], rest[1]
    qkv_refs = rest[2:2 + 3 * N_GROUPS]
    zbuf_ref, t_ref = rest[2 + 3 * N_GROUPS:]
    i = pl.program_id(0)
    u = _rms(x_ref[...], nm_ref[...]).astype(BF16)

    def mm(c0, n):
        return jnp.dot(u, w_ref[:, c0:c0 + n], preferred_element_type=F32)

    @pl.when((i * tm) % seq == 0)
    def _():
        zbuf_ref[0:8, :] = jnp.zeros((8, D_MODEL), F32)

    z = mm(COL_C, D_MODEL) * mm(COL_X, D_MODEL)
    zbuf_ref[8:tm + 8, :] = z
    cw = cw_ref[...]
    conv = cw[0:1] * zbuf_ref[6:tm + 6, :] + cw[1:2] * zbuf_ref[7:tm + 7, :] + cw[2:3] * z
    zbuf_ref[0:8, :] = zbuf_ref[tm:tm + 8, :]
    yb = (mm(COL_B, D_MODEL) * conv).astype(BF16)
    yc = jnp.dot(yb, wco_ref[...], preferred_element_type=F32)
    yc_ref[...] = (jax.nn.sigmoid(mm(COL_GC, D_MODEL) + gb_ref[0:1, :]) * yc).astype(BF16)
    ga_ref[...] = mm(COL_GA, D_MODEL).astype(BF16)

    ang = pos_ref[...].astype(F32) * invf_ref[...]
    cosv = jnp.cos(ang)
    sinv = jnp.sin(ang)
    lane = lax.broadcasted_iota(jnp.int32, (1, LANES), 1) % HEAD_DIM
    sin_lo = jnp.where(lane < ROT_HALF, -sinv, 0.0)
    sin_hi = jnp.where((lane >= ROT_HALF) & (lane < ROT_DIM), sinv, 0.0)
    for blk in range(3 * N_GROUPS):
        t = mm(COL_QKV + blk * D_GROUP, D_GROUP)
        dil = DILATED_GROUPS[blk % N_GROUPS][1]
        out_ref = qkv_refs[blk]
        for s in range(D_GROUP // LANES):
            ts = t[:, s * LANES:(s + 1) * LANES]
            if blk < 2 * N_GROUPS:
                ts = (ts * cosv + pltpu.roll(ts, LANES - ROT_HALF, 1) * sin_lo
                      + pltpu.roll(ts, ROT_HALF, 1) * sin_hi)
            if dil == 1:
                out_ref[0, :, s * LANES:(s + 1) * LANES] = ts.astype(BF16)
            else:
                t_ref[s] = ts
        if dil > 1:
            for r in range(dil):
                for s in range(D_GROUP // LANES):
                    out_ref[r, :, s * LANES:(s + 1) * LANES] = t_ref[
                        s, pl.ds(r, tm // dil, stride=dil), :].astype(BF16)


def _proj_call(x2, pos2, nm, invf, w_p, cw, gb, wco, *, seq):
    t = x2.shape[0]
    tm = TM_PROJ
    row = lambda n: pl.BlockSpec((tm, n), lambda i: (i, 0))
    dils = [d for _, d in DILATED_GROUPS] * 3
    qkv_specs = [pl.BlockSpec((d, tm // d, D_GROUP), lambda i: (0, i, 0)) for d in dils]
    qkv_shapes = [jax.ShapeDtypeStruct((d, t // d, D_GROUP), BF16) for d in dils]
    res = pl.pallas_call(
        functools.partial(_proj_kernel, tm=tm, seq=seq),
        grid=(t // tm,),
        in_specs=[row(D_MODEL), row(1), _const_spec((1, D_MODEL)), _const_spec((1, LANES)),
                  _const_spec((D_MODEL, D_IN)), _const_spec((3, D_MODEL)),
                  _const_spec((2, D_MODEL)), _const_spec((D_MODEL, D_MODEL))],
        out_specs=[row(D_MODEL), row(D_MODEL)] + qkv_specs,
        out_shape=[jax.ShapeDtypeStruct((t, D_MODEL), BF16),
                   jax.ShapeDtypeStruct((t, D_MODEL), BF16)] + qkv_shapes,
        scratch_shapes=[pltpu.VMEM((tm + 8, D_MODEL), F32),
                        pltpu.VMEM((D_GROUP // LANES, tm, LANES), F32)],
        compiler_params=pltpu.CompilerParams(dimension_semantics=("arbitrary",),
                                             vmem_limit_bytes=VMEM_LIMIT),
        name="proj",
    )(x2, pos2, nm, invf, w_p, cw, gb, wco)
    yc, ga = res[0], res[1]
    qs, ks, vs = res[2:2 + N_GROUPS], res[2 + N_GROUPS:2 + 2 * N_GROUPS], res[2 + 2 * N_GROUPS:]
    return yc, ga, qs, ks, vs


def _attn_kernel(q_ref, kp_ref, kc_ref, vp_ref, vc_ref, o_ref, lse_ref, *, steps, nsub):
    n = pl.program_id(2)
    qi = lax.broadcasted_iota(jnp.int32, (BLOCK, 2 * BLOCK), 0)
    ki = lax.broadcasted_iota(jnp.int32, (BLOCK, 2 * BLOCK), 1)
    dist = BLOCK + qi - ki
    band = (dist >= 0) & (dist <= steps)
    valid_first = band & ((n > 0) | (ki >= BLOCK))
    first = lax.broadcasted_iota(jnp.int32, (1, LANES), 1) < HEAD_DIM
    ones = jnp.ones((2 * BLOCK, LANES), BF16)
    scale = HEAD_DIM ** -0.5
    n_slab = D_GROUP // LANES
    slabs = [slice(hp * LANES, (hp + 1) * LANES) for hp in range(n_slab)]
    picks = (first, jnp.logical_not(first))

    def keys_of(prev_ref, cur_ref, sub, sl):
        if sub == 0:
            return jnp.concatenate([prev_ref[:, sl], cur_ref[0:BLOCK, sl]], axis=0)
        return cur_ref[(sub - 1) * BLOCK:(sub + 1) * BLOCK, sl]

    units = [(sub, hp) for sub in range(nsub) for hp in range(n_slab)]
    scores = []
    for sub, hp in units:
        sl = slabs[hp]
        q2 = q_ref[sub * BLOCK:(sub + 1) * BLOCK, sl] * scale
        k2 = keys_of(kp_ref, kc_ref, sub, sl)
        for pick in picks:
            qh = jnp.where(pick, q2, jnp.zeros((), BF16))
            s = lax.dot_general(qh, k2, (((1,), (1,)), ((), ())), preferred_element_type=F32)
            scores.append(jnp.where(valid_first if sub == 0 else band, s, -jnp.inf))
    maxes = [jnp.max(s, axis=-1, keepdims=True) for s in scores]
    probs = [jnp.exp(s - m).astype(BF16) for s, m in zip(scores, maxes)]
    head_of_lane = lax.broadcasted_iota(jnp.int32, (1, LANES), 1) // LSE_LANES
    lse = [None] * nsub
    for u, (sub, hp) in enumerate(units):
        sl = slabs[hp]
        rows = slice(sub * BLOCK, (sub + 1) * BLOCK)
        v2 = keys_of(vp_ref, vc_ref, sub, sl)
        pa, pb = probs[2 * u], probs[2 * u + 1]
        la = jnp.dot(pa, ones, preferred_element_type=F32)
        lb = jnp.dot(pb, ones, preferred_element_type=F32)
        o = jnp.where(first, jnp.dot(pa, v2, preferred_element_type=F32),
                      jnp.dot(pb, v2, preferred_element_type=F32))
        o_ref[rows, sl] = (o / jnp.where(first, la, lb)).astype(BF16)
        for k, (m, l) in enumerate(((maxes[2 * u], la), (maxes[2 * u + 1], lb))):
            val = m + jnp.log(l)
            lse[sub] = val if lse[sub] is None else jnp.where(
                head_of_lane == 2 * hp + k, val, lse[sub])
    for sub in range(nsub):
        lse_ref[sub * BLOCK:(sub + 1) * BLOCK, :] = lse[sub]


def _attn_call(q, k, v, g, *, batch, seq):
    window, dil = DILATED_GROUPS[g]
    steps = window // dil
    nsub = ATTN_BLOCKS_PER_STEP
    assert steps <= BLOCK and seq % (dil * BLOCK * nsub) == 0
    nb = seq // (dil * BLOCK)
    cur = pl.BlockSpec((None, nsub * BLOCK, D_GROUP), lambda b, r, n: (r, b * (nb // nsub) + n, 0))
    lse_spec = pl.BlockSpec((None, nsub * BLOCK, LANES), lambda b, r, n: (r, b * (nb // nsub) + n, 0))
    prev = pl.BlockSpec((None, BLOCK, D_GROUP),
                        lambda b, r, n: (r, b * nb + jnp.maximum(nsub * n - 1, 0), 0))
    return pl.pallas_call(
        functools.partial(_attn_kernel, steps=steps, nsub=nsub),
        grid=(batch, dil, nb // nsub),
        in_specs=[cur, prev, cur, prev, cur],
        out_specs=[cur, lse_spec],
        out_shape=[jax.ShapeDtypeStruct(q.shape, BF16),
                   jax.ShapeDtypeStruct(q.shape[:2] + (LANES,), F32)],
        compiler_params=pltpu.CompilerParams(
            dimension_semantics=("arbitrary", "arbitrary", "arbitrary")),
        name=f"attn{g}",
    )(q, k, k, v, v)


def _merge_kernel(x_ref, yc_ref, ga_ref, o0_ref, o1_ref, o2_ref, l0_ref, l1_ref, l2_ref,
                  gb_ref, wao_ref, wo_ref, h_ref, *scratch, tm):
    def natural(ref, g, scr, width):
        dil = DILATED_GROUPS[g][1]
        if dil == 1:
            return ref[0].astype(F32)
        n_slab = width // LANES
        for r in range(dil):
            for s in range(n_slab):
                scr[s, pl.ds(r, tm // dil, stride=dil), :] = ref[
                    r, :, s * LANES:(s + 1) * LANES].astype(F32)
        return jnp.concatenate([scr[s] for s in range(n_slab)], axis=1)

    scr = iter(scratch)
    o_refs, l_refs = (o0_ref, o1_ref, o2_ref), (l0_ref, l1_ref, l2_ref)
    dilated = [DILATED_GROUPS[g][1] > 1 for g in range(N_GROUPS)]
    ls = [natural(l_refs[g], g, next(scr) if dilated[g] else None, LANES)
          for g in range(N_GROUPS)]
    os_ = [natural(o_refs[g], g, next(scr) if dilated[g] else None, D_GROUP)
           for g in range(N_GROUPS)]
    mx = functools.reduce(jnp.maximum, ls)
    es = [jnp.exp(l - mx) for l in ls]
    inv = 1.0 / sum(es)
    row = lax.broadcasted_iota(jnp.int32, (LANES, D_GROUP), 0)
    col = lax.broadcasted_iota(jnp.int32, (LANES, D_GROUP), 1)
    spread = jnp.where(row == (col // HEAD_DIM) * LSE_LANES, 1.0, 0.0).astype(BF16)
    o = sum(jnp.dot((e * inv).astype(BF16), spread, preferred_element_type=F32) * ov
            for e, ov in zip(es, os_))
    ya = jnp.dot(o.astype(BF16), wao_ref[...], preferred_element_type=F32)
    merged = (yc_ref[...].astype(F32)
              + jax.nn.sigmoid(ga_ref[...].astype(F32) + gb_ref[1:2, :]) * ya)
    h_ref[...] = x_ref[...] + jnp.dot(merged.astype(BF16), wo_ref[...],
                                      preferred_element_type=F32)


def _merge_call(x2, yc, ga, outs, lses, gb, wao, wo):
    t = x2.shape[0]
    tm = TM_MERGE
    row = lambda n: pl.BlockSpec((tm, n), lambda i: (i, 0))
    dils = [d for _, d in DILATED_GROUPS]
    grp = lambda w: [pl.BlockSpec((d, tm // d, w), lambda i: (0, i, 0)) for d in dils]
    scratch = ([pltpu.VMEM((1, tm, LANES), F32) for d in dils if d > 1]
               + [pltpu.VMEM((D_GROUP // LANES, tm, LANES), F32) for d in dils if d > 1])
    return pl.pallas_call(
        functools.partial(_merge_kernel, tm=tm),
        grid=(t // tm,),
        in_specs=[row(D_MODEL), row(D_MODEL), row(D_MODEL)] + grp(D_GROUP) + grp(LANES)
                 + [_const_spec((2, D_MODEL)), _const_spec((D_GROUP, D_MODEL)),
                    _const_spec((D_MODEL, D_MODEL))],
        out_specs=row(D_MODEL),
        out_shape=jax.ShapeDtypeStruct((t, D_MODEL), F32),
        scratch_shapes=scratch,
        compiler_params=pltpu.CompilerParams(dimension_semantics=("arbitrary",),
                                             vmem_limit_bytes=VMEM_LIMIT),
        name="merge",
    )(x2, yc, ga, *outs, *lses, gb, wao, wo)


def _pair_candidates():
    return [(a, b) for a in range(PEER_TOPK) for b in range(PEER_TOPK)
            if (a + 1) * (b + 1) <= PEER_TOPK]


def _rank_code(r):
    return -RANK_CODE_BASE * (1.0 + r / 32.0)


def _route_kernel(h_ref, nf_ref, wqt_ref, sk_ref,
                  u2q_ref, n_ref, r2_ref, e1_ref, e2_ref,
                  qt_ref, ts_ref, s_ref, rk_ref, *, tm, ql):
    u2 = _rms(h_ref[...], nf_ref[...])
    u2t = u2.T.astype(BF16)
    for qq in range(tm // ql):
        u2q_ref[qq] = u2t[:, qq * ql:(qq + 1) * ql]
    qt_ref[...] = jnp.dot(wqt_ref[...], u2t, preferred_element_type=F32).astype(BF16)
    n_lb = tm // LANES

    def head_body(h, carry):
        for c in range(2):
            r0 = pl.multiple_of(h * (2 * PEER_D_HALF) + c * PEER_D_HALF, PEER_D_HALF)
            s_ref[c, h] = jnp.dot(sk_ref[h, c], qt_ref[pl.ds(r0, PEER_D_HALF), :],
                                  preferred_element_type=F32)
            for lb in range(n_lb):
                ln = slice(lb * LANES, (lb + 1) * LANES)
                cur = s_ref[c, h, :, ln]
                for r in range(PEER_TOPK):
                    m = jnp.max(cur, axis=0, keepdims=True)
                    ts_ref[c, r, lb, pl.ds(h, 1), :] = m
                    cur = jnp.where(cur >= m, _rank_code(r), cur)
                rk_ref[c, h, :, ln] = cur
        return carry

    lax.fori_loop(0, PEER_HEADS, head_body, 0)

    def top_rows(c, r):
        return jnp.concatenate([ts_ref[c, r, lb] for lb in range(n_lb)], axis=1)

    top1 = [top_rows(0, r) for r in range(PEER_TOPK)]
    top2 = [top_rows(1, r) for r in range(PEER_TOPK)]
    pairs = _pair_candidates()
    cands = [top1[a] + top2[b] for a, b in pairs]
    cmax = top1[0] + top2[0]
    cur = list(cands)
    for r in range(PEER_TOPK):
        m = functools.reduce(jnp.maximum, cur)
        if r + 1 < PEER_TOPK:
            cur = [jnp.where(c >= m, -jnp.inf, c) for c in cur]
    tau = m
    sel = [c >= tau for c in cands]
    z = functools.reduce(jnp.add, [jnp.where(s, jnp.exp(c - cmax), 0.0)
                                   for s, c in zip(sel, cands)])
    half_inv_z = 0.5 / z
    cnt = []
    for a in range(PEER_TOPK):
        cnt.append(functools.reduce(
            jnp.add, [jnp.where(s, 1.0, 0.0) for s, (pa, _) in zip(sel, pairs) if pa == a]))

    for h in range(PEER_HEADS):
        row = slice(h, h + 1)
        for lb in range(n_lb):
            ln = slice(lb * LANES, (lb + 1) * LANES)
            code1 = rk_ref[0, h, :, ln]
            n = jnp.zeros(code1.shape, F32)
            for a in range(PEER_TOPK):
                n = jnp.where(code1 == _rank_code(a), cnt[a][row, ln], n)
            n_ref[h, :, ln] = n
            e1_ref[h, :, ln] = jnp.exp(s_ref[0, h, :, ln] - top1[0][row, ln]) * half_inv_z[row, ln]
            code2 = rk_ref[1, h, :, ln]
            rank2 = jnp.where(code2 <= -RANK_CODE_BASE,
                              (code2 * (-1.0 / RANK_CODE_BASE) - 1.0) * 32.0, float(PEER_TOPK))
            r2_ref[h, :, ln] = rank2.astype(BF16)
            e2_ref[h, :, ln] = jnp.exp(s_ref[1, h, :, ln] - top2[0][row, ln]).astype(BF16)


def _route_call(h1, nf, wqt, sk):
    t = h1.shape[0]
    tm, ql = TM_ROUTE, QL_DENSE
    big = pl.BlockSpec((PEER_HEADS, PEER_N_KEYS, tm), lambda i: (0, 0, i))
    shape = lambda dt: jax.ShapeDtypeStruct((PEER_HEADS, PEER_N_KEYS, t), dt)
    return pl.pallas_call(
        functools.partial(_route_kernel, tm=tm, ql=ql),
        grid=(t // tm,),
        in_specs=[pl.BlockSpec((tm, D_MODEL), lambda i: (i, 0)), _const_spec((1, D_MODEL)),
                  _const_spec(wqt.shape), _const_spec(sk.shape)],
        out_specs=[pl.BlockSpec((tm // ql, D_MODEL, ql), lambda i: (i, 0, 0)),
                   big, big, big, big],
        out_shape=[jax.ShapeDtypeStruct((t // ql, D_MODEL, ql), BF16), shape(F32), shape(BF16),
                   shape(F32), shape(BF16)],
        scratch_shapes=[pltpu.VMEM((wqt.shape[0], tm), BF16),
                        pltpu.VMEM((2, PEER_TOPK, tm // LANES, PEER_HEADS, LANES), F32),
                        pltpu.VMEM((2, PEER_HEADS, PEER_N_KEYS, tm), F32),
                        pltpu.VMEM((2, PEER_HEADS, PEER_N_KEYS, tm), F32)],
        compiler_params=pltpu.CompilerParams(dimension_semantics=("arbitrary",),
                                             vmem_limit_bytes=VMEM_LIMIT),
        name="route",
    )(h1, nf, wqt, sk)


def _dense_kernel(u2q_ref, down_next_ref, down_first_ref, upt_ref,
                  n_ref, r2_ref, e1_ref, e2_ref, h_ref, fn_ref, out_ref,
                  acc_ref, ht_ref, wt_ref, *, tm, te, ql):
    i, j = pl.program_id(0), pl.program_id(1)
    last_j = pl.num_programs(1) - 1
    nc = te // PEER_N_KEYS
    nq = tm // ql

    def hidden(w_ref, q):
        ht_ref[q] = jnp.dot(w_ref[...], u2q_ref[q], preferred_element_type=F32)

    def project(w_ref, q):
        acc_ref[:, q * ql:(q + 1) * ql] += jnp.dot(w_ref[...], wt_ref[q],
                                                   preferred_element_type=F32)

    def key_rows(ref, h, cc, ln):
        tile = jnp.broadcast_to(ref[h, cc:cc + 1, ln], (BF16_ROWS, ql)).astype(BF16)
        return jnp.concatenate([tile] * (PEER_N_KEYS // BF16_ROWS), axis=0)

    @pl.when((i == 0) & (j == 0))
    def _():
        for q in range(nq):
            hidden(down_first_ref, q)

    @pl.when(j == 0)
    def _():
        acc_ref[...] = jnp.zeros_like(acc_ref)

    for q in range(nq):
        ln = slice(q * ql, (q + 1) * ql)
        for cc in range(nc):
            rows = slice(cc * PEER_N_KEYS, (cc + 1) * PEER_N_KEYS)
            g = jnp.zeros((PEER_N_KEYS, ql), BF16)
            for h in range(PEER_HEADS):
                gate = e2_ref[h, :, ln] * key_rows(e1_ref, h, cc, ln)
                g = g + jnp.where(r2_ref[h, :, ln] < key_rows(n_ref, h, cc, ln), gate,
                                  jnp.zeros((), BF16))
            hv = ht_ref[q, rows, :]
            act = hv * (1.0 + lax.erf(hv * np.float32(2.0 ** -0.5)))
            wt_ref[q, rows, :] = g * act.astype(BF16)
        project(upt_ref, q)
        hidden(down_next_ref, q)

    @pl.when(j == last_j)
    def _():
        out_ref[...] = _rms(h_ref[...] + acc_ref[...].T, fn_ref[...])


def _dense_call(u2q, down, upt, n, r2, e1, e2, h1, fn):
    t = h1.shape[0]
    n_exp = down.shape[0]
    tm, te, ql = TM_DENSE, TE_DENSE, QL_DENSE
    nc = te // PEER_N_KEYS
    ni, nj = t // tm, n_exp // te
    once = pl.Buffered(1)
    big = pl.BlockSpec((PEER_HEADS, PEER_N_KEYS, tm), lambda i, j: (0, 0, i))
    rows = pl.BlockSpec((PEER_HEADS, nc, tm), lambda i, j: (0, j, i))
    return pl.pallas_call(
        functools.partial(_dense_kernel, tm=tm, te=te, ql=ql),
        grid=(ni, nj),
        in_specs=[
            pl.BlockSpec((tm // ql, D_MODEL, ql),
                         lambda i, j: (jnp.minimum(i + (j + 1) // nj, ni - 1), 0, 0)),
            pl.BlockSpec((te, D_MODEL), lambda i, j: ((j + 1) % nj, 0)),
            pl.BlockSpec((te, D_MODEL), lambda i, j: (0, 0), pipeline_mode=once),
            pl.BlockSpec((D_MODEL, te), lambda i, j: (0, j)),
            rows, big, rows, big,
            pl.BlockSpec((tm, D_MODEL), lambda i, j: (i, 0)),
            pl.BlockSpec((1, D_MODEL), lambda i, j: (0, 0))],
        out_specs=pl.BlockSpec((tm, D_MODEL), lambda i, j: (i, 0)),
        out_shape=jax.ShapeDtypeStruct((t, D_MODEL), F32),
        scratch_shapes=[pltpu.VMEM((D_MODEL, tm), F32), pltpu.VMEM((tm // ql, te, ql), F32),
                        pltpu.VMEM((tm // ql, te, ql), BF16)],
        compiler_params=pltpu.CompilerParams(dimension_semantics=("arbitrary", "arbitrary"),
                                             vmem_limit_bytes=VMEM_LIMIT),
        name="dense",
    )(u2q, down, down, upt, n, r2, e1, e2, h1, fn)


def _rope_lane_freqs():
    inv_freq = ROPE_THETA ** (-jnp.arange(ROT_HALF, dtype=F32) * (2.0 / ROT_DIM))
    lane = np.arange(LANES) % HEAD_DIM
    return jnp.where(lane < ROT_DIM, inv_freq[lane % ROT_HALF], 0.0).reshape(1, LANES).astype(F32)


def kernel(x, positions, norm_mix, w_in, conv_w, w_conv_out, w_attn_out, gate_bias, w_out,
           norm_ffn, peer_w_query, peer_sub_keys, peer_down, peer_up, final_norm):
    batch, seq, d = x.shape
    assert d == D_MODEL and w_in.shape[-1] == D_IN
    depth = w_in.shape[0]
    t = batch * seq
    h = x.reshape(t, d)
    pos2 = positions.reshape(t, 1)
    invf = _rope_lane_freqs()
    for layer in range(depth):
        yc, ga, qs, ks, vs = _proj_call(h, pos2, norm_mix[layer].reshape(1, d), invf,
                                        w_in[layer].astype(BF16),
                                        conv_w[layer], gate_bias[layer],
                                        w_conv_out[layer].astype(BF16), seq=seq)
        outs, lses = zip(*[_attn_call(qs[g], ks[g], vs[g], g, batch=batch, seq=seq)
                           for g in range(N_GROUPS)])
        h1 = _merge_call(h, yc, ga, outs, lses, gate_bias[layer],
                         w_attn_out[layer].astype(BF16), w_out[layer].astype(BF16))
        u2q, n, r2, e1, e2 = _route_call(
            h1, norm_ffn[layer].reshape(1, d), peer_w_query[layer].T.astype(BF16),
            peer_sub_keys[layer].astype(BF16))
        assert layer == depth - 1, "intermediate layers need a dense call without the final norm"
        h = _dense_call(u2q, peer_down[layer].astype(BF16), peer_up[layer].T.astype(BF16),
                        n, r2, e1, e2, h1, final_norm.reshape(1, d))
    return h.reshape(batch, seq, d)
```

```python
import functools

import numpy as np
import jax
import jax.numpy as jnp
from jax import lax
from jax.experimental import pallas as pl
from jax.experimental.pallas import tpu as pltpu

F32 = jnp.float32
BF16 = jnp.bfloat16

D_MODEL = 1024
HEAD_DIM = 64
HEADS_PER_GROUP = 8
DILATED_GROUPS = ((128, 1), (512, 4), (2048, 16))
N_GROUPS = len(DILATED_GROUPS)
D_GROUP = HEADS_PER_GROUP * HEAD_DIM
D_ATTN = N_GROUPS * D_GROUP
BLOCK = 128
ROT_DIM = HEAD_DIM // 4
ROT_HALF = ROT_DIM // 2
ROPE_THETA = 500000.0
PEER_HEADS = 8
PEER_N_KEYS = 128
PEER_TOPK = 16
PEER_D_HALF = 128
RMS_EPS = 1e-6
LANES = 128
SUBLANES = 8
BF16_ROWS = 2 * SUBLANES
RANK_CODE_BASE = 2.0 ** 120
VMEM_LIMIT = 60 * 1024 * 1024

COL_B, COL_C, COL_X, COL_QKV = 0, D_MODEL, 2 * D_MODEL, 3 * D_MODEL
COL_GC = COL_QKV + 3 * D_ATTN
COL_GA = COL_GC + D_MODEL
D_IN = COL_GA + D_MODEL

LSE_LANES = LANES // HEADS_PER_GROUP
ATTN_BLOCKS_PER_STEP = 4
TM_PROJ = 512
TM_MERGE = 512
TM_ROUTE = 512
TM_DENSE = 512
TE_DENSE = SUBLANES * PEER_N_KEYS
QL_DENSE = 256


def _const_spec(shape):
    nd = len(shape)
    return pl.BlockSpec(shape, lambda *_: (0,) * nd, pipeline_mode=pl.Buffered(1))


def _rms(x, g):
    return x * lax.rsqrt(jnp.mean(x * x, axis=-1, keepdims=True) + RMS_EPS) * g


def _proj_kernel(x_ref, pos_ref, nm_ref, invf_ref, w_ref, cw_ref, gb_ref, wco_ref, *rest,
                 tm, seq):
    yc_ref, ga_ref = rest[0], rest[1]
    qkv_refs = rest[2:2 + 3 * N_GROUPS]
    zbuf_ref, t_ref = rest[2 + 3 * N_GROUPS:]
    i = pl.program_id(0)
    u = _rms(x_ref[...], nm_ref[...]).astype(BF16)

    def mm(c0, n):
        return jnp.dot(u, w_ref[:, c0:c0 + n], preferred_element_type=F32)

    @pl.when((i * tm) % seq == 0)
    def _():
        zbuf_ref[0:8, :] = jnp.zeros((8, D_MODEL), F32)

    z = mm(COL_C, D_MODEL) * mm(COL_X, D_MODEL)
    zbuf_ref[8:tm + 8, :] = z
    cw = cw_ref[...]
    conv = cw[0:1] * zbuf_ref[6:tm + 6, :] + cw[1:2] * zbuf_ref[7:tm + 7, :] + cw[2:3] * z
    zbuf_ref[0:8, :] = zbuf_ref[tm:tm + 8, :]
    yb = (mm(COL_B, D_MODEL) * conv).astype(BF16)
    yc = jnp.dot(yb, wco_ref[...], preferred_element_type=F32)
    yc_ref[...] = (jax.nn.sigmoid(mm(COL_GC, D_MODEL) + gb_ref[0:1, :]) * yc).astype(BF16)
    ga_ref[...] = mm(COL_GA, D_MODEL).astype(BF16)

    ang = pos_ref[...].astype(F32) * invf_ref[...]
    cosv = jnp.cos(ang)
    sinv = jnp.sin(ang)
    lane = lax.broadcasted_iota(jnp.int32, (1, LANES), 1) % HEAD_DIM
    sin_lo = jnp.where(lane < ROT_HALF, -sinv, 0.0)
    sin_hi = jnp.where((lane >= ROT_HALF) & (lane < ROT_DIM), sinv, 0.0)
    for blk in range(3 * N_GROUPS):
        t = mm(COL_QKV + blk * D_GROUP, D_GROUP)
        dil = DILATED_GROUPS[blk % N_GROUPS][1]
        out_ref = qkv_refs[blk]
        for s in range(D_GROUP // LANES):
            ts = t[:, s * LANES:(s + 1) * LANES]
            if blk < 2 * N_GROUPS:
                ts = (ts * cosv + pltpu.roll(ts, LANES - ROT_HALF, 1) * sin_lo
                      + pltpu.roll(ts, ROT_HALF, 1) * sin_hi)
            if dil == 1:
                out_ref[0, :, s * LANES:(s + 1) * LANES] = ts.astype(BF16)
            else:
                t_ref[s] = ts
        if dil > 1:
            for r in range(dil):
                for s in range(D_GROUP // LANES):
                    out_ref[r, :, s * LANES:(s + 1) * LANES] = t_ref[
                        s, pl.ds(r, tm // dil, stride=dil), :].astype(BF16)


def _proj_call(x2, pos2, nm, invf, w_p, cw, gb, wco, *, seq):
    t = x2.shape[0]
    tm = TM_PROJ
    row = lambda n: pl.BlockSpec((tm, n), lambda i: (i, 0))
    dils = [d for _, d in DILATED_GROUPS] * 3
    qkv_specs = [pl.BlockSpec((d, tm // d, D_GROUP), lambda i: (0, i, 0)) for d in dils]
    qkv_shapes = [jax.ShapeDtypeStruct((d, t // d, D_GROUP), BF16) for d in dils]
    res = pl.pallas_call(
        functools.partial(_proj_kernel, tm=tm, seq=seq),
        grid=(t // tm,),
        in_specs=[row(D_MODEL), row(1), _const_spec((1, D_MODEL)), _const_spec((1, LANES)),
                  _const_spec((D_MODEL, D_IN)), _const_spec((3, D_MODEL)),
                  _const_spec((2, D_MODEL)), _const_spec((D_MODEL, D_MODEL))],
        out_specs=[row(D_MODEL), row(D_MODEL)] + qkv_specs,
        out_shape=[jax.ShapeDtypeStruct((t, D_MODEL), BF16),
                   jax.ShapeDtypeStruct((t, D_MODEL), BF16)] + qkv_shapes,
        scratch_shapes=[pltpu.VMEM((tm + 8, D_MODEL), F32),
                        pltpu.VMEM((D_GROUP // LANES, tm, LANES), F32)],
        compiler_params=pltpu.CompilerParams(dimension_semantics=("arbitrary",),
                                             vmem_limit_bytes=VMEM_LIMIT),
        name="proj",
    )(x2, pos2, nm, invf, w_p, cw, gb, wco)
    yc, ga = res[0], res[1]
    qs, ks, vs = res[2:2 + N_GROUPS], res[2 + N_GROUPS:2 + 2 * N_GROUPS], res[2 + 2 * N_GROUPS:]
    return yc, ga, qs, ks, vs


def _attn_kernel(q_ref, kp_ref, kc_ref, vp_ref, vc_ref, o_ref, lse_ref, *, steps, nsub):
    n = pl.program_id(2)
    qi = lax.broadcasted_iota(jnp.int32, (BLOCK, 2 * BLOCK), 0)
    ki = lax.broadcasted_iota(jnp.int32, (BLOCK, 2 * BLOCK), 1)
    dist = BLOCK + qi - ki
    band = (dist >= 0) & (dist <= steps)
    valid_first = band & ((n > 0) | (ki >= BLOCK))
    first = lax.broadcasted_iota(jnp.int32, (1, LANES), 1) < HEAD_DIM
    ones = jnp.ones((2 * BLOCK, LANES), BF16)
    scale = HEAD_DIM ** -0.5
    n_slab = D_GROUP // LANES
    slabs = [slice(hp * LANES, (hp + 1) * LANES) for hp in range(n_slab)]
    picks = (first, jnp.logical_not(first))

    def keys_of(prev_ref, cur_ref, sub, sl):
        if sub == 0:
            return jnp.concatenate([prev_ref[:, sl], cur_ref[0:BLOCK, sl]], axis=0)
        return cur_ref[(sub - 1) * BLOCK:(sub + 1) * BLOCK, sl]

    units = [(sub, hp) for sub in range(nsub) for hp in range(n_slab)]
    scores = []
    for sub, hp in units:
        sl = slabs[hp]
        q2 = q_ref[sub * BLOCK:(sub + 1) * BLOCK, sl] * scale
        k2 = keys_of(kp_ref, kc_ref, sub, sl)
        for pick in picks:
            qh = jnp.where(pick, q2, jnp.zeros((), BF16))
            s = lax.dot_general(qh, k2, (((1,), (1,)), ((), ())), preferred_element_type=F32)
            scores.append(jnp.where(valid_first if sub == 0 else band, s, -jnp.inf))
    maxes = [jnp.max(s, axis=-1, keepdims=True) for s in scores]
    probs = [jnp.exp(s - m).astype(BF16) for s, m in zip(scores, maxes)]
    head_of_lane = lax.broadcasted_iota(jnp.int32, (1, LANES), 1) // LSE_LANES
    lse = [None] * nsub
    for u, (sub, hp) in enumerate(units):
        sl = slabs[hp]
        rows = slice(sub * BLOCK, (sub + 1) * BLOCK)
        v2 = keys_of(vp_ref, vc_ref, sub, sl)
        pa, pb = probs[2 * u], probs[2 * u + 1]
        la = jnp.dot(pa, ones, preferred_element_type=F32)
        lb = jnp.dot(pb, ones, preferred_element_type=F32)
        o = jnp.where(first, jnp.dot(pa, v2, preferred_element_type=F32),
                      jnp.dot(pb, v2, preferred_element_type=F32))
        o_ref[rows, sl] = (o / jnp.where(first, la, lb)).astype(BF16)
        for k, (m, l) in enumerate(((maxes[2 * u], la), (maxes[2 * u + 1], lb))):
            val = m + jnp.log(l)
            lse[sub] = val if lse[sub] is None else jnp.where(
                head_of_lane == 2 * hp + k, val, lse[sub])
    for sub in range(nsub):
        lse_ref[sub * BLOCK:(sub + 1) * BLOCK, :] = lse[sub]


def _attn_call(q, k, v, g, *, batch, seq):
    window, dil = DILATED_GROUPS[g]
    steps = window // dil
    nsub = ATTN_BLOCKS_PER_STEP
    assert steps <= BLOCK and seq % (dil * BLOCK * nsub) == 0
    nb = seq // (dil * BLOCK)
    cur = pl.BlockSpec((None, nsub * BLOCK, D_GROUP), lambda b, r, n: (r, b * (nb // nsub) + n, 0))
    lse_spec = pl.BlockSpec((None, nsub * BLOCK, LANES), lambda b, r, n: (r, b * (nb // nsub) + n, 0))
    prev = pl.BlockSpec((None, BLOCK, D_GROUP),
                        lambda b, r, n: (r, b * nb + jnp.maximum(nsub * n - 1, 0), 0))
    return pl.pallas_call(
        functools.partial(_attn_kernel, steps=steps, nsub=nsub),
        grid=(batch, dil, nb // nsub),
        in_specs=[cur, prev, cur, prev, cur],
        out_specs=[cur, lse_spec],
        out_shape=[jax.ShapeDtypeStruct(q.shape, BF16),
                   jax.ShapeDtypeStruct(q.shape[:2] + (LANES,), F32)],
        compiler_params=pltpu.CompilerParams(
            dimension_semantics=("arbitrary", "arbitrary", "arbitrary")),
        name=f"attn{g}",
    )(q, k, k, v, v)


def _merge_kernel(x_ref, yc_ref, ga_ref, o0_ref, o1_ref, o2_ref, l0_ref, l1_ref, l2_ref,
                  gb_ref, wao_ref, wo_ref, h_ref, *scratch, tm):
    def natural(ref, g, scr, width):
        dil = DILATED_GROUPS[g][1]
        if dil == 1:
            return ref[0].astype(F32)
        n_slab = width // LANES
        for r in range(dil):
            for s in range(n_slab):
                scr[s, pl.ds(r, tm // dil, stride=dil), :] = ref[
                    r, :, s * LANES:(s + 1) * LANES].astype(F32)
        return jnp.concatenate([scr[s] for s in range(n_slab)], axis=1)

    scr = iter(scratch)
    o_refs, l_refs = (o0_ref, o1_ref, o2_ref), (l0_ref, l1_ref, l2_ref)
    dilated = [DILATED_GROUPS[g][1] > 1 for g in range(N_GROUPS)]
    ls = [natural(l_refs[g], g, next(scr) if dilated[g] else None, LANES)
          for g in range(N_GROUPS)]
    os_ = [natural(o_refs[g], g, next(scr) if dilated[g] else None, D_GROUP)
           for g in range(N_GROUPS)]
    mx = functools.reduce(jnp.maximum, ls)
    es = [jnp.exp(l - mx) for l in ls]
    inv = 1.0 / sum(es)
    row = lax.broadcasted_iota(jnp.int32, (LANES, D_GROUP), 0)
    col = lax.broadcasted_iota(jnp.int32, (LANES, D_GROUP), 1)
    spread = jnp.where(row == (col // HEAD_DIM) * LSE_LANES, 1.0, 0.0).astype(BF16)
    o = sum(jnp.dot((e * inv).astype(BF16), spread, preferred_element_type=F32) * ov
            for e, ov in zip(es, os_))
    ya = jnp.dot(o.astype(BF16), wao_ref[...], preferred_element_type=F32)
    merged = (yc_ref[...].astype(F32)
              + jax.nn.sigmoid(ga_ref[...].astype(F32) + gb_ref[1:2, :]) * ya)
    h_ref[...] = x_ref[...] + jnp.dot(merged.astype(BF16), wo_ref[...],
                                      preferred_element_type=F32)


def _merge_call(x2, yc, ga, outs, lses, gb, wao, wo):
    t = x2.shape[0]
    tm = TM_MERGE
    row = lambda n: pl.BlockSpec((tm, n), lambda i: (i, 0))
    dils = [d for _, d in DILATED_GROUPS]
    grp = lambda w: [pl.BlockSpec((d, tm // d, w), lambda i: (0, i, 0)) for d in dils]
    scratch = ([pltpu.VMEM((1, tm, LANES), F32) for d in dils if d > 1]
               + [pltpu.VMEM((D_GROUP // LANES, tm, LANES), F32) for d in dils if d > 1])
    return pl.pallas_call(
        functools.partial(_merge_kernel, tm=tm),
        grid=(t // tm,),
        in_specs=[row(D_MODEL), row(D_MODEL), row(D_MODEL)] + grp(D_GROUP) + grp(LANES)
                 + [_const_spec((2, D_MODEL)), _const_spec((D_GROUP, D_MODEL)),
                    _const_spec((D_MODEL, D_MODEL))],
        out_specs=row(D_MODEL),
        out_shape=jax.ShapeDtypeStruct((t, D_MODEL), F32),
        scratch_shapes=scratch,
        compiler_params=pltpu.CompilerParams(dimension_semantics=("arbitrary",),
                                             vmem_limit_bytes=VMEM_LIMIT),
        name="merge",
    )(x2, yc, ga, *outs, *lses, gb, wao, wo)


def _pair_candidates():
    return [(a, b) for a in range(PEER_TOPK) for b in range(PEER_TOPK)
            if (a + 1) * (b + 1) <= PEER_TOPK]


def _rank_code(r):
    return -RANK_CODE_BASE * (1.0 + r / 32.0)


def _route_kernel(h_ref, h_next_ref, nf_ref, wqt_ref, sk_ref,
                  u2q_ref, n_ref, r2_ref, e1_ref, e2_ref,
                  qt_ref, u2c_ref, ts_ref, s_ref, rk_ref, *, tm, ql):
    def prepare(src_ref):
        u2 = _rms(src_ref[...], nf_ref[...])
        u2t = u2.T.astype(BF16)
        for qq in range(tm // ql):
            u2c_ref[qq] = u2t[:, qq * ql:(qq + 1) * ql]
        qt_ref[...] = jnp.dot(wqt_ref[...], u2t, preferred_element_type=F32).astype(BF16)

    @pl.when(pl.program_id(0) == 0)
    def _():
        prepare(h_ref)

    u2q_ref[...] = u2c_ref[...]
    n_lb = tm // LANES

    def head_body(h, carry):
        for c in range(2):
            r0 = pl.multiple_of(h * (2 * PEER_D_HALF) + c * PEER_D_HALF, PEER_D_HALF)
            s_ref[c, h] = jnp.dot(sk_ref[h, c], qt_ref[pl.ds(r0, PEER_D_HALF), :],
                                  preferred_element_type=F32)
            for lb in range(n_lb):
                ln = slice(lb * LANES, (lb + 1) * LANES)
                cur = s_ref[c, h, :, ln]
                for r in range(PEER_TOPK):
                    m = jnp.max(cur, axis=0, keepdims=True)
                    ts_ref[c, r, lb, pl.ds(h, 1), :] = m
                    cur = jnp.where(cur >= m, _rank_code(r), cur)
                rk_ref[c, h, :, ln] = cur
        return carry

    lax.fori_loop(0, PEER_HEADS, head_body, 0)

    def top_rows(c, r):
        return jnp.concatenate([ts_ref[c, r, lb] for lb in range(n_lb)], axis=1)

    top1 = [top_rows(0, r) for r in range(PEER_TOPK)]
    top2 = [top_rows(1, r) for r in range(PEER_TOPK)]
    pairs = _pair_candidates()
    cands = [top1[a] + top2[b] for a, b in pairs]
    cmax = top1[0] + top2[0]
    cur = list(cands)
    for r in range(PEER_TOPK):
        m = functools.reduce(jnp.maximum, cur)
        if r + 1 < PEER_TOPK:
            cur = [jnp.where(c >= m, -jnp.inf, c) for c in cur]
    tau = m
    sel = [c >= tau for c in cands]
    z = functools.reduce(jnp.add, [jnp.where(s, jnp.exp(c - cmax), 0.0)
                                   for s, c in zip(sel, cands)])
    half_inv_z = 0.5 / z
    cnt = []
    for a in range(PEER_TOPK):
        cnt.append(functools.reduce(
            jnp.add, [jnp.where(s, 1.0, 0.0) for s, (pa, _) in zip(sel, pairs) if pa == a]))

    for h in range(PEER_HEADS):
        row = slice(h, h + 1)
        for lb in range(n_lb):
            ln = slice(lb * LANES, (lb + 1) * LANES)
            code1 = rk_ref[0, h, :, ln]
            n = jnp.zeros(code1.shape, F32)
            for a in range(PEER_TOPK):
                n = jnp.where(code1 == _rank_code(a), cnt[a][row, ln], n)
            n_ref[h, :, ln] = n
            e1_ref[h, :, ln] = jnp.exp(s_ref[0, h, :, ln] - top1[0][row, ln]) * half_inv_z[row, ln]
            code2 = rk_ref[1, h, :, ln]
            rank2 = jnp.where(code2 <= -RANK_CODE_BASE,
                              (code2 * (-1.0 / RANK_CODE_BASE) - 1.0) * 32.0, float(PEER_TOPK))
            r2_ref[h, :, ln] = rank2.astype(BF16)
            e2_ref[h, :, ln] = jnp.exp(s_ref[1, h, :, ln] - top2[0][row, ln]).astype(BF16)
    prepare(h_next_ref)


def _route_call(h1, nf, wqt, sk):
    t = h1.shape[0]
    tm, ql = TM_ROUTE, QL_DENSE
    nt = t // tm
    big = pl.BlockSpec((PEER_HEADS, PEER_N_KEYS, tm), lambda i: (0, 0, i))
    shape = lambda dt: jax.ShapeDtypeStruct((PEER_HEADS, PEER_N_KEYS, t), dt)
    return pl.pallas_call(
        functools.partial(_route_kernel, tm=tm, ql=ql),
        grid=(nt,),
        in_specs=[pl.BlockSpec((tm, D_MODEL), lambda i: (i, 0)),
                  pl.BlockSpec((tm, D_MODEL), lambda i: (jnp.minimum(i + 1, nt - 1), 0)),
                  _const_spec((1, D_MODEL)), _const_spec(wqt.shape), _const_spec(sk.shape)],
        out_specs=[pl.BlockSpec((tm // ql, D_MODEL, ql), lambda i: (i, 0, 0)),
                   big, big, big, big],
        out_shape=[jax.ShapeDtypeStruct((t // ql, D_MODEL, ql), BF16), shape(F32), shape(BF16),
                   shape(F32), shape(BF16)],
        scratch_shapes=[pltpu.VMEM((wqt.shape[0], tm), BF16),
                        pltpu.VMEM((tm // ql, D_MODEL, ql), BF16),
                        pltpu.VMEM((2, PEER_TOPK, tm // LANES, PEER_HEADS, LANES), F32),
                        pltpu.VMEM((2, PEER_HEADS, PEER_N_KEYS, tm), F32),
                        pltpu.VMEM((2, PEER_HEADS, PEER_N_KEYS, tm), F32)],
        compiler_params=pltpu.CompilerParams(dimension_semantics=("arbitrary",),
                                             vmem_limit_bytes=VMEM_LIMIT),
        name="route",
    )(h1, h1, nf, wqt, sk)


def _dense_kernel(u2q_ref, down_next_ref, down_first_ref, upt_ref,
                  n_ref, r2_ref, e1_ref, e2_ref, h_ref, fn_ref, out_ref,
                  acc_ref, ht_ref, wt_ref, *, tm, te, ql):
    i, j = pl.program_id(0), pl.program_id(1)
    last_j = pl.num_programs(1) - 1
    nc = te // PEER_N_KEYS
    nq = tm // ql

    def hidden(w_ref, q):
        ht_ref[q] = jnp.dot(w_ref[...], u2q_ref[q], preferred_element_type=F32)

    def project(w_ref, q):
        acc_ref[:, q * ql:(q + 1) * ql] += jnp.dot(w_ref[...], wt_ref[q],
                                                   preferred_element_type=F32)

    def key_rows(ref, h, cc, ln):
        tile = jnp.broadcast_to(ref[h, cc:cc + 1, ln], (BF16_ROWS, ql)).astype(BF16)
        return jnp.concatenate([tile] * (PEER_N_KEYS // BF16_ROWS), axis=0)

    @pl.when((i == 0) & (j == 0))
    def _():
        for q in range(nq):
            hidden(down_first_ref, q)

    @pl.when(j == 0)
    def _():
        acc_ref[...] = jnp.zeros_like(acc_ref)

    for q in range(nq):
        ln = slice(q * ql, (q + 1) * ql)
        for cc in range(nc):
            rows = slice(cc * PEER_N_KEYS, (cc + 1) * PEER_N_KEYS)
            g = jnp.zeros((PEER_N_KEYS, ql), BF16)
            for h in range(PEER_HEADS):
                gate = e2_ref[h, :, ln] * key_rows(e1_ref, h, cc, ln)
                g = g + jnp.where(r2_ref[h, :, ln] < key_rows(n_ref, h, cc, ln), gate,
                                  jnp.zeros((), BF16))
            hv = ht_ref[q, rows, :]
            act = hv * (1.0 + lax.erf(hv * np.float32(2.0 ** -0.5)))
            wt_ref[q, rows, :] = g * act.astype(BF16)
        project(upt_ref, q)
        hidden(down_next_ref, q)

    @pl.when(j == last_j)
    def _():
        out_ref[...] = _rms(h_ref[...] + acc_ref[...].T, fn_ref[...])


def _dense_call(u2q, down, upt, n, r2, e1, e2, h1, fn):
    t = h1.shape[0]
    n_exp = down.shape[0]
    tm, te, ql = TM_DENSE, TE_DENSE, QL_DENSE
    nc = te // PEER_N_KEYS
    ni, nj = t // tm, n_exp // te
    once = pl.Buffered(1)
    big = pl.BlockSpec((PEER_HEADS, PEER_N_KEYS, tm), lambda i, j: (0, 0, i))
    rows = pl.BlockSpec((PEER_HEADS, nc, tm), lambda i, j: (0, j, i))
    return pl.pallas_call(
        functools.partial(_dense_kernel, tm=tm, te=te, ql=ql),
        grid=(ni, nj),
        in_specs=[
            pl.BlockSpec((tm // ql, D_MODEL, ql),
                         lambda i, j: (jnp.minimum(i + (j + 1) // nj, ni - 1), 0, 0)),
            pl.BlockSpec((te, D_MODEL), lambda i, j: ((j + 1) % nj, 0)),
            pl.BlockSpec((te, D_MODEL), lambda i, j: (0, 0), pipeline_mode=once),
            pl.BlockSpec((D_MODEL, te), lambda i, j: (0, j)),
            rows, big, rows, big,
            pl.BlockSpec((tm, D_MODEL), lambda i, j: (i, 0)),
            pl.BlockSpec((1, D_MODEL), lambda i, j: (0, 0))],
        out_specs=pl.BlockSpec((tm, D_MODEL), lambda i, j: (i, 0)),
        out_shape=jax.ShapeDtypeStruct((t, D_MODEL), F32),
        scratch_shapes=[pltpu.VMEM((D_MODEL, tm), F32), pltpu.VMEM((tm // ql, te, ql), F32),
                        pltpu.VMEM((tm // ql, te, ql), BF16)],
        compiler_params=pltpu.CompilerParams(dimension_semantics=("arbitrary", "arbitrary"),
                                             vmem_limit_bytes=VMEM_LIMIT),
        name="dense",
    )(u2q, down, down, upt, n, r2, e1, e2, h1, fn)


def _rope_lane_freqs():
    inv_freq = ROPE_THETA ** (-jnp.arange(ROT_HALF, dtype=F32) * (2.0 / ROT_DIM))
    lane = np.arange(LANES) % HEAD_DIM
    return jnp.where(lane < ROT_DIM, inv_freq[lane % ROT_HALF], 0.0).reshape(1, LANES).astype(F32)


def kernel(x, positions, norm_mix, w_in, conv_w, w_conv_out, w_attn_out, gate_bias, w_out,
           norm_ffn, peer_w_query, peer_sub_keys, peer_down, peer_up, final_norm):
    batch, seq, d = x.shape
    assert d == D_MODEL and w_in.shape[-1] == D_IN
    depth = w_in.shape[0]
    t = batch * seq
    h = x.reshape(t, d)
    pos2 = positions.reshape(t, 1)
    invf = _rope_lane_freqs()
    for layer in range(depth):
        yc, ga, qs, ks, vs = _proj_call(h, pos2, norm_mix[layer].reshape(1, d), invf,
                                        w_in[layer].astype(BF16),
                                        conv_w[layer], gate_bias[layer],
                                        w_conv_out[layer].astype(BF16), seq=seq)
        outs, lses = zip(*[_attn_call(qs[g], ks[g], vs[g], g, batch=batch, seq=seq)
                           for g in range(N_GROUPS)])
        h1 = _merge_call(h, yc, ga, outs, lses, gate_bias[layer],
                         w_attn_out[layer].astype(BF16), w_out[layer].astype(BF16))
        u2q, n, r2, e1, e2 = _route_call(
            h1, norm_ffn[layer].reshape(1, d), peer_w_query[layer].T.astype(BF16),
            peer_sub_keys[layer].astype(BF16))
        assert layer == depth - 1, "intermediate layers need a dense call without the final norm"
        h = _dense_call(u2q, peer_down[layer].astype(BF16), peer_up[layer].T.astype(BF16),
                        n, r2, e1, e2, h1, final_norm.reshape(1, d))
    return h.reshape(batch, seq, d)
```
